```python
import numpy as np
import jax, jax.numpy as jnp
from jax import lax

D_MODEL = 2048
BATCH = 8
SEQ = 2048
DEPTH = 2
DEC_BATCH = 2
DEC_SEQ = 4096
PAST_LEN = 128

HEAD_DIM = 64
A_PATTERNS = ((128, 1), (512, 4), (2048, 16))
A_HEADS_PER_GROUP = 4
A_HEADS = A_HEADS_PER_GROUP * len(A_PATTERNS)
A_WIDTH = A_HEADS * HEAD_DIM
A_OUT = A_HEADS_PER_GROUP * HEAD_DIM
A_BLOCK = 64
B_WIDTH = 512
B_BLOCKS = 8
B_BLOCK_W = B_WIDTH // B_BLOCKS
CONV_W = 4
RG_C = 8.0
C_HEADS = 12
C_WIDTH = C_HEADS * HEAD_DIM
GRID_W = 64
NA_KH = 8
NA_KW = 16
D_FF = -(-8 * D_MODEL // (3 * 256)) * 256
D_IN = 3 * A_WIDTH + 2 * B_WIDTH + 3 * C_WIDTH
D_MIX_OUT = A_OUT + B_WIDTH + C_WIDTH
NORM_EPS = 1e-6

kernel_name = 'hybrid_dilated_rglru_natten_encoder'


def _rmsnorm(x, g):
    x32 = x.astype(jnp.float32)
    y = x32 * lax.rsqrt(jnp.mean(x32 * x32, axis=-1, keepdims=True) + NORM_EPS)
    return (y * g.astype(jnp.float32)).astype(x.dtype)


def _alibi_slopes(n):
    return 2.0 ** (-8.0 * jnp.arange(1, n + 1, dtype=jnp.float32) / n)


def _to_strided(x, d):
    B, T = x.shape[:2]
    rest = x.shape[2:]
    x = x.reshape((B, T // d, d) + rest)
    perm = (0, 2, 1) + tuple(range(3, x.ndim))
    return x.transpose(perm).reshape((B * d, T // d) + rest)


def _from_strided(x, B, d):
    N, n = x.shape[:2]
    rest = x.shape[2:]
    x = x.reshape((B, d, n) + rest)
    perm = (0, 2, 1) + tuple(range(3, x.ndim))
    return x.transpose(perm).reshape((B, n * d) + rest)


def _strided_window_attention(q, k, v, slopes, dil, half):
    N, n, H, dh = q.shape
    blk = A_BLOCK
    nb = -(-n // blk)
    pad = nb * blk - n
    qb = jnp.pad(q, ((0, 0), (0, pad), (0, 0), (0, 0))).reshape(N, nb, blk, H, dh)

    def key_blocks(t):
        tp = jnp.pad(t, ((0, 0), (blk, blk + pad), (0, 0), (0, 0))).reshape(N, nb + 2, blk, H, dh)
        return jnp.concatenate([tp[:, :-2], tp[:, 1:-1], tp[:, 2:]], axis=2)

    kb, vb = key_blocks(k), key_blocks(v)
    qpos = jnp.arange(nb)[:, None] * blk + jnp.arange(blk)[None, :]
    kpos = jnp.arange(nb)[:, None] * blk - blk + jnp.arange(3 * blk)[None, :]
    rel = jnp.abs(kpos[:, None, :] - qpos[:, :, None])
    valid = (rel <= half) & (kpos[:, None, :] >= 0) & (kpos[:, None, :] < n)
    s = jnp.einsum('nbqhd,nbkhd->nbhqk', qb, kb).astype(jnp.float32) * (HEAD_DIM ** -0.5)
    s = s - (slopes * dil)[:, None, None] * rel[:, None].astype(jnp.float32)
    s = jnp.where(valid[:, None], s, -jnp.inf)
    m = jnp.max(s, axis=-1, keepdims=True)
    p = jnp.exp(s - m)
    den = jnp.sum(p, axis=-1, keepdims=True)
    o = jnp.einsum('nbhqk,nbkhd->nbqhd', (p / den).astype(v.dtype), vb)
    lse = (m + jnp.log(den))[..., 0]
    o = o.reshape(N, nb * blk, H, dh)[:, :n]
    lse = lse.transpose(0, 1, 3, 2).reshape(N, nb * blk, H)[:, :n]
    return o, lse


def _mixer_dilated(q, k, v):
    B, T = q.shape[:2]
    slopes = _alibi_slopes(A_HEADS)
    outs, lses = [], []
    for g, (win, dil) in enumerate(A_PATTERNS):
        hs = slice(g * A_HEADS_PER_GROUP, (g + 1) * A_HEADS_PER_GROUP)
        half = (win // 2) // dil
        o, lse = _strided_window_attention(_to_strided(q[:, :, hs], dil), _to_strided(k[:, :, hs], dil),
                                           _to_strided(v[:, :, hs], dil), slopes[hs], dil, half)
        outs.append(_from_strided(o, B, dil))
        lses.append(_from_strided(lse, B, dil))
    w = jax.nn.softmax(jnp.stack(lses), axis=0)
    o = jnp.sum(w[..., None] * jnp.stack(outs).astype(jnp.float32), axis=0)
    return o.reshape(B, T, A_OUT).astype(q.dtype)


def _lin_combine(e1, e2):
    a1, b1 = e1
    a2, b2 = e2
    return a1 * a2, a2 * b1 + b2


def _rglru_scan(xc, wa, ba, wx, bx, lam, reverse):
    B, T, W = xc.shape
    xb = xc.reshape(B, T, B_BLOCKS, B_BLOCK_W)
    r = jax.nn.sigmoid((jnp.einsum('btnc,ncd->btnd', xb, wa).reshape(B, T, W) + ba).astype(jnp.float32))
    i = jax.nn.sigmoid((jnp.einsum('btnc,ncd->btnd', xb, wx).reshape(B, T, W) + bx).astype(jnp.float32))
    log_a = -RG_C * r * jax.nn.softplus(-lam.astype(jnp.float32))
    a = jnp.exp(log_a)
    b = jnp.sqrt(-jnp.expm1(2.0 * log_a)) * (i * xc.astype(jnp.float32))
    _, h = lax.associative_scan(_lin_combine, (a, b), reverse=reverse, axis=1)
    return h


def _mixer_rglru(gate_in, x_in, conv_w, conv_b, wa, ba, wx, bx, lam):
    T = x_in.shape[1]
    left = CONV_W // 2
    xp = jnp.pad(x_in, ((0, 0), (left, CONV_W - 1 - left), (0, 0)))
    xc = conv_b + conv_w[0] * xp[:, 0:T]
    for j in range(1, CONV_W):
        xc = xc + conv_w[j] * xp[:, j:j + T]
    h = (_rglru_scan(xc, wa[0], ba[0], wx[0], bx[0], lam[0], False)
         + _rglru_scan(xc, wa[1], ba[1], wx[1], bx[1], lam[1], True))
    return jax.nn.gelu(gate_in) * h.astype(gate_in.dtype)


def _na_col_tables():
    n_blk = GRID_W // NA_KW
    span = 2 * NA_KW
    c = np.arange(GRID_W).reshape(n_blk, NA_KW)
    cs = np.clip(c - NA_KW // 2, 0, GRID_W - NA_KW)
    starts = np.clip(np.arange(n_blk) * NA_KW - NA_KW // 2, 0, GRID_W - span)
    kcol = starts[:, None] + np.arange(span)[None, :]
    mask = (kcol[:, None, :] >= cs[:, :, None]) & (kcol[:, None, :] < cs[:, :, None] + NA_KW)
    off = np.clip(kcol[:, None, :] - c[:, :, None] + NA_KW - 1, 0, 2 * NA_KW - 2)
    return kcol, mask, off


def _mixer_neighbourhood(q, k, v, rpb):
    B, T, H, dh = q.shape
    rows = T // GRID_W
    kh = min(NA_KH, rows)
    kcol, cmask, coff = _na_col_tables()
    n_blk, span = kcol.shape
    mask = jnp.asarray(np.broadcast_to(cmask[:, :, None, :], (n_blk, NA_KW, kh, span))
                       .reshape(n_blk, 1, NA_KW, kh * span))
    qg = q.reshape(B, rows, GRID_W, H, dh)
    kg = k.reshape(B, rows, GRID_W, H, dh)
    vg = v.reshape(B, rows, GRID_W, H, dh)
    rpb32 = rpb.astype(jnp.float32)
    scale = HEAD_DIM ** -0.5

    def row_fn(r):
        rs = jnp.clip(r - kh // 2, 0, rows - kh)
        q_r = lax.dynamic_index_in_dim(qg, r, axis=1, keepdims=False).reshape(B, n_blk, NA_KW, H, dh)

        def gather(t):
            t_r = lax.dynamic_slice_in_dim(t, rs, kh, axis=1)[:, :, kcol]
            return t_r.transpose(0, 2, 1, 3, 4, 5).reshape(B, n_blk, kh * span, H, dh)

        k_r, v_r = gather(kg), gather(vg)
        s = jnp.einsum('bjqhd,bjkhd->bjhqk', q_r, k_r).astype(jnp.float32) * scale
        row_idx = rs + jnp.arange(kh) - r + NA_KH - 1
        bias = rpb32[:, row_idx][:, :, coff]
        bias = bias.transpose(2, 0, 3, 1, 4).reshape(n_blk, H, NA_KW, kh * span)
        s = jnp.where(mask, s + bias, -jnp.inf)
        p = jax.nn.softmax(s, axis=-1)
        o = jnp.einsum('bjhqk,bjkhd->bjqhd', p.astype(v.dtype), v_r)
        return o.reshape(B, GRID_W, H, dh)

    out = lax.map(row_fn, jnp.arange(rows))
    return out.transpose(1, 0, 2, 3, 4).reshape(B, T, H * dh)


def _encode(x, norm1_g, w_in, conv_w, conv_b, rg_wa, rg_ba, rg_wx, rg_bx, rg_lam, na_rpb,
            w_out, norm2_g, w_ffn_in, w_ffn_out, final_g):
    B, T, _ = x.shape
    splits = list(np.cumsum([A_WIDTH, A_WIDTH, A_WIDTH, B_WIDTH, B_WIDTH, C_WIDTH, C_WIDTH]))
    for l in range(DEPTH):
        xn = _rmsnorm(x, norm1_g[l])
        proj = jnp.einsum('btd,de->bte', xn, w_in[l])
        qa, ka, va, gb, xb, qc, kc, vc = jnp.split(proj, splits, axis=-1)
        heads_a = lambda t: t.reshape(B, T, A_HEADS, HEAD_DIM)
        heads_c = lambda t: t.reshape(B, T, C_HEADS, HEAD_DIM)
        o_a = _mixer_dilated(heads_a(qa), heads_a(ka), heads_a(va))
        o_b = _mixer_rglru(gb, xb, conv_w[l], conv_b[l], rg_wa[l], rg_ba[l], rg_wx[l], rg_bx[l], rg_lam[l])
        o_c = _mixer_neighbourhood(heads_c(qc), heads_c(kc), heads_c(vc), na_rpb[l])
        mix = jnp.concatenate([o_a, o_b, o_c], axis=-1)
        x = x + jnp.einsum('bte,ed->btd', mix, w_out[l])
        hn = _rmsnorm(x, norm2_g[l])
        gate, up = jnp.split(jnp.einsum('btd,df->btf', hn, w_ffn_in[l]), 2, axis=-1)
        x = x + jnp.einsum('btf,fd->btd', jax.nn.silu(gate) * up, w_ffn_out[l])
    return _rmsnorm(x, final_g)


def setup_inputs(seed: int = 0) -> dict:
    key = jax.random.key(seed)
    ks = jax.random.split(key, 20)
    f32 = jnp.float32
    nrm = lambda k, shape, s: jax.random.normal(k, shape, f32) * s
    u = jax.random.uniform(ks[10], (DEPTH, 2, B_WIDTH), f32, minval=0.9, maxval=0.999)
    a0 = u ** (1.0 / RG_C)
    return {
        'x_prompt': nrm(ks[0], (BATCH, SEQ, D_MODEL), 1.0),
        'x_sample': nrm(ks[1], (DEC_BATCH, DEC_SEQ, D_MODEL), 1.0),
        'norm1_g': 1.0 + nrm(ks[2], (DEPTH, D_MODEL), 0.02),
        'w_in': nrm(ks[3], (DEPTH, D_MODEL, D_IN), D_MODEL ** -0.5),
        'conv_w': nrm(ks[4], (DEPTH, CONV_W, B_WIDTH), CONV_W ** -0.5),
        'conv_b': nrm(ks[5], (DEPTH, B_WIDTH), 0.02),
        'rg_wa': nrm(ks[6], (DEPTH, 2, B_BLOCKS, B_BLOCK_W, B_BLOCK_W), B_BLOCK_W ** -0.5),
        'rg_ba': nrm(ks[7], (DEPTH, 2, B_WIDTH), 0.02),
        'rg_wx': nrm(ks[8], (DEPTH, 2, B_BLOCKS, B_BLOCK_W, B_BLOCK_W), B_BLOCK_W ** -0.5),
        'rg_bx': nrm(ks[9], (DEPTH, 2, B_WIDTH), 0.02),
        'rg_lam': jnp.log(a0) - jnp.log1p(-a0),
        'na_rpb': nrm(ks[11], (DEPTH, C_HEADS, 2 * NA_KH - 1, 2 * NA_KW - 1), 0.1),
        'w_out': nrm(ks[12], (DEPTH, D_MIX_OUT, D_MODEL), D_MIX_OUT ** -0.5),
        'norm2_g': 1.0 + nrm(ks[13], (DEPTH, D_MODEL), 0.02),
        'w_ffn_in': nrm(ks[14], (DEPTH, D_MODEL, 2 * D_FF), D_MODEL ** -0.5),
        'w_ffn_out': nrm(ks[15], (DEPTH, D_FF, D_MODEL), D_FF ** -0.5),
        'final_g': 1.0 + nrm(ks[16], (D_MODEL,), 0.02),
    }


def reference(x_prompt, x_sample, norm1_g, w_in, conv_w, conv_b, rg_wa, rg_ba, rg_wx, rg_bx,
              rg_lam, na_rpb, w_out, norm2_g, w_ffn_in, w_ffn_out, final_g):
    y_prompt = _encode(x_prompt, norm1_g, w_in, conv_w, conv_b, rg_wa, rg_ba, rg_wx, rg_bx, rg_lam,
                       na_rpb, w_out, norm2_g, w_ffn_in, w_ffn_out, final_g)
    y_sample = _encode(x_sample, norm1_g, w_in, conv_w, conv_b, rg_wa, rg_ba, rg_wx, rg_bx, rg_lam,
                       na_rpb, w_out, norm2_g, w_ffn_in, w_ffn_out, final_g)
    return (y_prompt, y_sample)
```

```python
import functools

import numpy as np
import jax
import jax.numpy as jnp
from jax import lax
from jax.experimental import pallas as pl
from jax.experimental.pallas import tpu as pltpu

D_MODEL = 2048
DEPTH = 2
HEAD_DIM = 64
A_PATTERNS = ((128, 1), (512, 4), (2048, 16))
A_HEADS_PER_GROUP = 4
A_HEADS = A_HEADS_PER_GROUP * len(A_PATTERNS)
A_WIDTH = A_HEADS * HEAD_DIM
A_OUT = A_HEADS_PER_GROUP * HEAD_DIM
B_WIDTH = 512
B_BLOCKS = 8
CONV_W = 4
RG_C = 8.0
C_HEADS = 12
C_WIDTH = C_HEADS * HEAD_DIM
GRID_W = 64
NA_KH = 8
NA_KW = 16
D_FF = 5632
D_IN = 3 * A_WIDTH + 2 * B_WIDTH + 3 * C_WIDTH
D_MIX_OUT = A_OUT + B_WIDTH + C_WIDTH
NORM_EPS = 1e-6

F32 = jnp.float32
BF16 = jnp.bfloat16

HEAD_GROUP_LANES = 4 * HEAD_DIM
SCALE = HEAD_DIM ** -0.5
NEG = -1e30

COL_QA, COL_KA, COL_VA = 0, 3, 6
COL_GB, COL_XB = 9, 11
COL_QC, COL_KC, COL_VC = 13, 16, 19

VMEM_LIMIT = 56 * 1024 * 1024


def _params(sem):
    return pltpu.CompilerParams(dimension_semantics=sem, vmem_limit_bytes=VMEM_LIMIT)


def _rmsnorm_rows(x, g):
    ms = jnp.mean(x * x, axis=-1, keepdims=True)
    return x * lax.rsqrt(ms + NORM_EPS) * g


def _norm_proj_kernel(x_ref, g_ref, w_ref, o_ref, xn_ref):
    @pl.when(pl.program_id(1) == 0)
    def _():
        xn_ref[...] = _rmsnorm_rows(x_ref[...], g_ref[...]).astype(BF16)

    o_ref[...] = jnp.dot(xn_ref[...], w_ref[...], preferred_element_type=F32).astype(o_ref.dtype)


def _norm_proj(x, g, w, tm, tn):
    m, d = x.shape
    n = w.shape[1]
    return pl.pallas_call(
        _norm_proj_kernel,
        grid=(m // tm, n // tn),
        in_specs=[
            pl.BlockSpec((tm, d), lambda i, j: (i, 0)),
            pl.BlockSpec((1, d), lambda i, j: (0, 0)),
            pl.BlockSpec((d, tn), lambda i, j: (0, j)),
        ],
        out_specs=pl.BlockSpec((tm, tn), lambda i, j: (i, j)),
        out_shape=jax.ShapeDtypeStruct((m, n), BF16),
        scratch_shapes=[pltpu.VMEM((tm, d), BF16)],
        compiler_params=_params(("parallel", "arbitrary")),
        name="norm_proj",
    )(x, g.reshape(1, d), w)


def _norm_swiglu_kernel(x_ref, g_ref, wg_ref, wu_ref, o_ref, xn_ref):
    @pl.when(pl.program_id(1) == 0)
    def _():
        xn_ref[...] = _rmsnorm_rows(x_ref[...], g_ref[...]).astype(BF16)

    xn = xn_ref[...]
    gate = jnp.dot(xn, wg_ref[...], preferred_element_type=F32)
    up = jnp.dot(xn, wu_ref[...], preferred_element_type=F32)
    o_ref[...] = (gate * jax.nn.sigmoid(gate) * up).astype(o_ref.dtype)


def _norm_swiglu(x, g, w, tm, tn):
    m, d = x.shape
    f = w.shape[1] // 2
    nf = f // tn
    return pl.pallas_call(
        _norm_swiglu_kernel,
        grid=(m // tm, nf),
        in_specs=[
            pl.BlockSpec((tm, d), lambda i, j: (i, 0)),
            pl.BlockSpec((1, d), lambda i, j: (0, 0)),
            pl.BlockSpec((d, tn), lambda i, j: (0, j)),
            pl.BlockSpec((d, tn), lambda i, j: (0, j + nf)),
        ],
        out_specs=pl.BlockSpec((tm, tn), lambda i, j: (i, j)),
        out_shape=jax.ShapeDtypeStruct((m, f), BF16),
        scratch_shapes=[pltpu.VMEM((tm, d), BF16)],
        compiler_params=_params(("parallel", "arbitrary")),
        name="norm_swiglu",
    )(x, g.reshape(1, d), w, w)


def _mix_out_kernel(res_ref, a_ref, b_ref, c_ref, w_ref, o_ref):
    acc = res_ref[...]
    acc = acc + jnp.dot(a_ref[...], w_ref[0:A_OUT, :], preferred_element_type=F32)
    acc = acc + jnp.dot(b_ref[...], w_ref[A_OUT:A_OUT + B_WIDTH, :], preferred_element_type=F32)
    acc = acc + jnp.dot(c_ref[...], w_ref[A_OUT + B_WIDTH:, :], preferred_element_type=F32)
    o_ref[...] = acc


def _mix_out(res, o_a, o_b, o_c, w, tm, tn):
    m, d = res.shape
    return pl.pallas_call(
        _mix_out_kernel,
        grid=(m // tm, d // tn),
        in_specs=[
            pl.BlockSpec((tm, tn), lambda i, j: (i, j)),
            pl.BlockSpec((tm, A_OUT), lambda i, j: (i, 0)),
            pl.BlockSpec((tm, B_WIDTH), lambda i, j: (i, 0)),
            pl.BlockSpec((tm, C_WIDTH), lambda i, j: (i, 0)),
            pl.BlockSpec((D_MIX_OUT, tn), lambda i, j: (0, j)),
        ],
        out_specs=pl.BlockSpec((tm, tn), lambda i, j: (i, j)),
        out_shape=jax.ShapeDtypeStruct((m, d), F32),
        compiler_params=_params(("parallel", "arbitrary")),
        name="mix_out",
    )(res, o_a, o_b, o_c, w)


def _ffn_out_kernel(res_ref, a_ref, w_ref, o_ref):
    o_ref[...] = res_ref[...] + jnp.dot(a_ref[...], w_ref[...], preferred_element_type=F32)


def _ffn_out(res, act, w, tm, tn):
    m, d = res.shape
    f = act.shape[1]
    return pl.pallas_call(
        _ffn_out_kernel,
        grid=(m // tm, d // tn),
        in_specs=[
            pl.BlockSpec((tm, tn), lambda i, j: (i, j)),
            pl.BlockSpec((tm, f), lambda i, j: (i, 0)),
            pl.BlockSpec((f, tn), lambda i, j: (0, j)),
        ],
        out_specs=pl.BlockSpec((tm, tn), lambda i, j: (i, j)),
        out_shape=jax.ShapeDtypeStruct((m, d), F32),
        compiler_params=_params(("parallel", "arbitrary")),
        name="ffn_out",
    )(res, act, w)


def _final_norm_kernel(x_ref, g_ref, o_ref):
    o_ref[...] = _rmsnorm_rows(x_ref[...], g_ref[...])


def _final_norm(x, g, tm):
    m, d = x.shape
    return pl.pallas_call(
        _final_norm_kernel,
        grid=(m // tm,),
        in_specs=[pl.BlockSpec((tm, d), lambda i: (i, 0)), pl.BlockSpec((1, d), lambda i: (0, 0))],
        out_specs=pl.BlockSpec((tm, d), lambda i: (i, 0)),
        out_shape=jax.ShapeDtypeStruct((m, d), F32),
        compiler_params=_params(("parallel",)),
        name="final_norm",
    )(x, g.reshape(1, d))


def _lane_head(rows):
    return lax.shift_right_logical(lax.broadcasted_iota(jnp.int32, (rows, HEAD_GROUP_LANES), 1), 6)


def _stack_heads(q, lane_head):
    qf = q.astype(F32) * SCALE
    return jnp.concatenate(
        [jnp.where(lane_head == h, qf, 0.0) for h in range(A_HEADS_PER_GROUP)], axis=0
    ).astype(BF16)


def _alibi_slope(h):
    return 2.0 ** (-8.0 * (h + 1) / A_HEADS)


def _dilated_kernel(*refs, n, dil, half, group, tq, nk, n_prev):
    q_ref, k_ref, v_ref = refs[:3]
    prev = refs[3:3 + 2 * n_prev]
    outs = refs[3 + 2 * n_prev:]
    lane_head = _lane_head(tq)
    row_iota = lax.broadcasted_iota(jnp.int32, (tq, nk), 0)
    col_iota = lax.broadcasted_iota(jnp.int32, (tq, nk), 1)

    def body(i, carry):
        q0 = pl.multiple_of(i * tq, tq)
        start = pl.multiple_of(jnp.clip(q0 - half, 0, n - nk), half) if n > nk else 0
        qs = _stack_heads(q_ref[pl.ds(q0, tq), :], lane_head)
        kw = k_ref[pl.ds(start, nk), :]
        vw = v_ref[pl.ds(start, nk), :]
        s = lax.dot_general(qs, kw, (((1,), (1,)), ((), ())), preferred_element_type=F32)
        rel = jnp.abs((col_iota + start) - (row_iota + q0))
        valid = rel <= half
        relf = rel.astype(F32)
        o = jnp.zeros((tq, HEAD_GROUP_LANES), F32)
        lse = jnp.zeros((tq, HEAD_GROUP_LANES), F32)
        for h in range(A_HEADS_PER_GROUP):
            coef = _alibi_slope(group * A_HEADS_PER_GROUP + h) * dil
            sh = jnp.where(valid, s[h * tq:(h + 1) * tq] - coef * relf, NEG)
            m = jnp.max(sh, axis=-1, keepdims=True)
            p = jnp.exp(sh - m)
            l = jnp.sum(p, axis=-1, keepdims=True)
            pv = jnp.dot(p.astype(BF16), vw, preferred_element_type=F32) / l
            o = jnp.where(lane_head == h, pv, o)
            lse = jnp.where(lane_head == h, m + jnp.log(l), lse)
        if n_prev == 0 or len(outs) == 2:
            outs[0][pl.ds(q0, tq), :] = o
            outs[1][pl.ds(q0, tq), :] = lse
        else:
            os_ = [prev[2 * g][pl.ds(q0, tq), :] for g in range(n_prev)] + [o]
            ls_ = [prev[2 * g + 1][pl.ds(q0, tq), :] for g in range(n_prev)] + [lse]
            mx = functools.reduce(jnp.maximum, ls_)
            ws = [jnp.exp(l_ - mx) for l_ in ls_]
            num = functools.reduce(lambda a, b: a + b, [w_ * o_ for w_, o_ in zip(ws, os_)])
            den = functools.reduce(lambda a, b: a + b, ws)
            outs[0][pl.ds(q0, tq), :] = (num / den).astype(outs[0].dtype)
        return carry

    lax.fori_loop(0, n // tq, body, 0)


def _mixer_dilated(proj3):
    bsz, t, _ = proj3.shape
    qkv_a = proj3[:, :, :3 * A_WIDTH]
    prev = []
    n_groups = len(A_PATTERNS)
    for g, (win, dil) in enumerate(A_PATTERNS):
        n = t // dil
        half = (win // 2) // dil
        tq = min(2 * half, n)
        nk = min(4 * half, n)
        src = proj3 if dil == 1 else qkv_a.reshape(bsz, n, dil * 3 * A_WIDTH)
        ncol = src.shape[-1] // (dil * HEAD_GROUP_LANES)

        def col_spec(off, g=g, n=n, ncol=ncol):
            return pl.BlockSpec((None, n, HEAD_GROUP_LANES), lambda b, r: (b, 0, r * ncol + off + g))

        tok_spec = pl.BlockSpec((None, n, HEAD_GROUP_LANES), lambda b, r: (b, 0, r))
        last = g == n_groups - 1
        prev_in = [a.reshape(bsz, n, dil * A_OUT) for a in prev] if last else []
        if last:
            out_shape = jax.ShapeDtypeStruct((bsz, n, dil * A_OUT), BF16)
            out_specs = tok_spec
        else:
            out_shape = (jax.ShapeDtypeStruct((bsz, n, dil * A_OUT), F32),) * 2
            out_specs = (tok_spec, tok_spec)
        res = pl.pallas_call(
            functools.partial(_dilated_kernel, n=n, dil=dil, half=half, group=g, tq=tq, nk=nk,
                              n_prev=len(prev_in) // 2),
            grid=(bsz, dil),
            in_specs=[col_spec(COL_QA), col_spec(COL_KA), col_spec(COL_VA)] + [tok_spec] * len(prev_in),
            out_specs=out_specs,
            out_shape=out_shape,
            compiler_params=_params(("parallel", "parallel")),
            name=f"dilated_attn_g{g}",
        )(src, src, src, *prev_in)
        if last:
            return res.reshape(bsz, t, A_OUT)
        prev += [res[0].reshape(bsz, t, A_OUT), res[1].reshape(bsz, t, A_OUT)]


B_TILE = 256
B_CHUNK = 256
B_PAD = 8
SUBLANES = 8


def _scan8(a, b, reverse):
    row = lax.broadcasted_iota(jnp.int32, a.shape, 0)
    for sh in (1, 2, 4):
        if reverse:
            a_s = pltpu.roll(a, SUBLANES - sh, 0)
            b_s = pltpu.roll(b, SUBLANES - sh, 0)
            ok = row < SUBLANES - sh
        else:
            a_s = pltpu.roll(a, sh, 0)
            b_s = pltpu.roll(b, sh, 0)
            ok = row >= sh
        a_s = jnp.where(ok, a_s, 1.0)
        b_s = jnp.where(ok, b_s, 0.0)
        b = a * b_s + b
        a = a * a_s
    return a, b


def _softplus(z):
    return jnp.maximum(z, 0.0) + jnp.log1p(jnp.exp(-jnp.abs(z)))


def _rglru_kernel(gb_ref, xb_ref, wg_ref, pk_ref, o_ref, xs_ref, af_ref, bf_ref, ab_ref, bb_ref, *, t):
    ct = B_TILE
    pk = pk_ref[...]
    conv_w = [pk[j:j + 1, :] for j in range(CONV_W)]
    conv_b = pk[4:5, :]
    bias = [pk[5 + j:6 + j, :] for j in range(4)]
    sp = [_softplus(-pk[9 + j:10 + j, :]) for j in range(2)]

    xs_ref[0:B_PAD, :] = jnp.zeros((B_PAD, ct), F32)
    xs_ref[B_PAD + t:2 * B_PAD + t, :] = jnp.zeros((B_PAD, ct), F32)

    def fill(c, carry):
        c0 = pl.multiple_of(c * B_CHUNK, B_CHUNK)
        xs_ref[pl.ds(c0 + B_PAD, B_CHUNK), :] = xb_ref[pl.ds(c0, B_CHUNK), :].astype(F32)
        return carry

    lax.fori_loop(0, t // B_CHUNK, fill, 0)

    def gates(c, carry):
        c0 = pl.multiple_of(c * B_CHUNK, B_CHUNK)
        slab = xs_ref[pl.ds(c0, B_CHUNK + 2 * B_PAD), :]
        left = CONV_W // 2
        xc = conv_b + conv_w[0] * slab[B_PAD - left:B_PAD - left + B_CHUNK]
        for j in range(1, CONV_W):
            xc = xc + conv_w[j] * slab[B_PAD - left + j:B_PAD - left + j + B_CHUNK]
        g = jnp.dot(xc.astype(BF16), wg_ref[...], preferred_element_type=F32)
        for d, (a_ref, b_ref) in enumerate(((af_ref, bf_ref), (ab_ref, bb_ref))):
            r = jax.nn.sigmoid(g[:, (2 * d) * ct:(2 * d + 1) * ct] + bias[2 * d])
            i = jax.nn.sigmoid(g[:, (2 * d + 1) * ct:(2 * d + 2) * ct] + bias[2 * d + 1])
            log_a = -RG_C * r * sp[d]
            a = jnp.exp(log_a)
            one_minus_a2 = -jnp.tanh(log_a) * (a * a + 1.0)
            a_ref[pl.ds(c0, B_CHUNK), :] = a
            b_ref[pl.ds(c0, B_CHUNK), :] = jnp.sqrt(one_minus_a2) * (i * xc)
        return carry

    lax.fori_loop(0, t // B_CHUNK, gates, 0)

    def scan(k, carry):
        cf, cb = carry
        rf = pl.multiple_of(k * SUBLANES, SUBLANES)
        rb = pl.multiple_of(t - SUBLANES - k * SUBLANES, SUBLANES)
        a, b = _scan8(af_ref[pl.ds(rf, SUBLANES), :], bf_ref[pl.ds(rf, SUBLANES), :], False)
        hf = a * cf + b
        bf_ref[pl.ds(rf, SUBLANES), :] = hf
        a, b = _scan8(ab_ref[pl.ds(rb, SUBLANES), :], bb_ref[pl.ds(rb, SUBLANES), :], True)
        hb = a * cb + b
        bb_ref[pl.ds(rb, SUBLANES), :] = hb
        return hf[SUBLANES - 1:SUBLANES, :], hb[0:1, :]

    zero = jnp.zeros((1, ct), F32)
    lax.fori_loop(0, t // SUBLANES, scan, (zero, zero))

    def emit(c, carry):
        c0 = pl.multiple_of(c * B_CHUNK, B_CHUNK)
        h = bf_ref[pl.ds(c0, B_CHUNK), :] + bb_ref[pl.ds(c0, B_CHUNK), :]
        gate = gb_ref[pl.ds(c0, B_CHUNK), :].astype(F32)
        o_ref[pl.ds(c0, B_CHUNK), :] = (jax.nn.gelu(gate) * h).astype(o_ref.dtype)
        return carry

    lax.fori_loop(0, t // B_CHUNK, emit, 0)


def _rglru_weights(conv_w, conv_b, wa, ba, wx, bx, lam):
    eye = jnp.eye(B_BLOCKS, dtype=F32)

    def dense(w):
        return jnp.einsum("ncd,nm->ncmd", w, eye).reshape(B_WIDTH, B_WIDTH)

    mats = [dense(wa[0]), dense(wx[0]), dense(wa[1]), dense(wx[1])]
    nt = B_WIDTH // B_TILE
    wg = jnp.stack([
        jnp.concatenate([m[c * B_TILE:(c + 1) * B_TILE, c * B_TILE:(c + 1) * B_TILE] for m in mats], axis=1)
        for c in range(nt)
    ]).astype(BF16)
    rows = [conv_w[j] for j in range(CONV_W)] + [conv_b, ba[0], bx[0], ba[1], bx[1], lam[0], lam[1]]
    rows += [jnp.zeros_like(conv_b)] * (16 - len(rows))
    pk = jnp.stack(rows).reshape(16, nt, B_TILE).transpose(1, 0, 2)
    return wg, pk


def _mixer_rglru(proj3, wg, pk):
    bsz, t, _ = proj3.shape
    nt = B_WIDTH // B_TILE
    return pl.pallas_call(
        functools.partial(_rglru_kernel, t=t),
        grid=(bsz, nt),
        in_specs=[
            pl.BlockSpec((None, t, B_TILE), lambda b, c: (b, 0, COL_GB + c)),
            pl.BlockSpec((None, t, B_TILE), lambda b, c: (b, 0, COL_XB + c)),
            pl.BlockSpec((None, B_TILE, 4 * B_TILE), lambda b, c: (c, 0, 0)),
            pl.BlockSpec((None, 16, B_TILE), lambda b, c: (c, 0, 0)),
        ],
        out_specs=pl.BlockSpec((None, t, B_TILE), lambda b, c: (b, 0, c)),
        out_shape=jax.ShapeDtypeStruct((bsz, t, B_WIDTH), BF16),
        scratch_shapes=[pltpu.VMEM((t + 2 * B_PAD, B_TILE), F32)] + [pltpu.VMEM((t, B_TILE), F32)] * 4,
        compiler_params=_params(("parallel", "parallel")),
        name="rglru",
    )(proj3, proj3, wg, pk)


NA_KEYS = NA_KH * GRID_W


def _na_bias_tables(rpb):
    c = np.arange(GRID_W)
    cs = np.clip(c - NA_KW // 2, 0, GRID_W - NA_KW)
    kc = np.arange(GRID_W)
    inwin = (kc[None, :] >= cs[:, None]) & (kc[None, :] < cs[:, None] + NA_KW)
    off = np.clip(kc[None, :] - c[:, None] + NA_KW - 1, 0, 2 * NA_KW - 2)
    rowidx = np.arange(NA_KH)[:, None] + np.arange(NA_KH)[None, :]
    tb = rpb.astype(F32)[:, rowidx][:, :, :, off]
    tb = jnp.where(inwin[None, None, None], tb, NEG)
    tb = tb.transpose(0, 1, 3, 2, 4).reshape(C_HEADS // 4, 4, NA_KH, GRID_W, NA_KEYS)
    return tb.transpose(0, 2, 1, 3, 4).reshape(C_HEADS // 4, NA_KH, 4 * GRID_W, NA_KEYS)


def _na_kernel(q_ref, k_ref, v_ref, tbl_ref, o_ref, *, rows):
    lane_head = _lane_head(GRID_W)

    def body(r, carry):
        rs = jnp.clip(r - NA_KH // 2, 0, rows - NA_KH)
        dr0 = rs - r + NA_KH - 1
        q0 = pl.multiple_of(r * GRID_W, GRID_W)
        k0 = pl.multiple_of(rs * GRID_W, GRID_W)
        qs = _stack_heads(q_ref[pl.ds(q0, GRID_W), :], lane_head)
        kw = k_ref[pl.ds(k0, NA_KEYS), :]
        vw = v_ref[pl.ds(k0, NA_KEYS), :]
        s = lax.dot_general(qs, kw, (((1,), (1,)), ((), ())), preferred_element_type=F32)
        s = s + tbl_ref[dr0]
        m = jnp.max(s, axis=-1, keepdims=True)
        p = jnp.exp(s - m)
        l = jnp.sum(p, axis=-1, keepdims=True)
        pv = jnp.dot(p.astype(BF16), vw, preferred_element_type=F32) / l
        o = pv[0:GRID_W]
        for h in range(1, 4):
            o = jnp.where(lane_head == h, pv[h * GRID_W:(h + 1) * GRID_W], o)
        o_ref[pl.ds(q0, GRID_W), :] = o.astype(o_ref.dtype)
        return carry

    lax.fori_loop(0, rows, body, 0)


def _mixer_na(proj3, tbl):
    bsz, t, _ = proj3.shape
    rows = t // GRID_W
    nhg = C_HEADS // 4

    def col_spec(off):
        return pl.BlockSpec((None, t, HEAD_GROUP_LANES), lambda hg, b: (b, 0, off + hg))

    return pl.pallas_call(
        functools.partial(_na_kernel, rows=rows),
        grid=(nhg, bsz),
        in_specs=[col_spec(COL_QC), col_spec(COL_KC), col_spec(COL_VC),
                  pl.BlockSpec((None, NA_KH, 4 * GRID_W, NA_KEYS), lambda hg, b: (hg, 0, 0, 0))],
        out_specs=pl.BlockSpec((None, t, HEAD_GROUP_LANES), lambda hg, b: (b, 0, hg)),
        out_shape=jax.ShapeDtypeStruct((bsz, t, C_WIDTH), BF16),
        compiler_params=_params(("parallel", "parallel")),
        name="na_attn",
    )(proj3, proj3, proj3, tbl)


def _tile(m, want):
    return want if m % want == 0 else m


def _encode(x, w):
    bsz, t, d = x.shape
    m = bsz * t
    tm = _tile(m, 1024)
    xf = x.reshape(m, d)
    for l in range(DEPTH):
        proj = _norm_proj(xf, w["norm1_g"][l], w["w_in"][l], tm, 512)
        proj3 = proj.reshape(bsz, t, D_IN)
        o_a = _mixer_dilated(proj3)
        o_b = _mixer_rglru(proj3, w["rg_wg"][l], w["rg_pk"][l])
        o_c = _mixer_na(proj3, w["na_tbl"][l])
        xf = _mix_out(xf, o_a.reshape(m, A_OUT), o_b.reshape(m, B_WIDTH), o_c.reshape(m, C_WIDTH),
                      w["w_out"][l], tm, 512)
        act = _norm_swiglu(xf, w["norm2_g"][l], w["w_ffn_in"][l], tm, 512)
        xf = _ffn_out(xf, act, w["w_ffn_out"][l], _tile(m, 512), 512)
    return _final_norm(xf, w["final_g"], _tile(m, 512)).reshape(bsz, t, d)


def kernel(x_prompt, x_sample, norm1_g, w_in, conv_w, conv_b, rg_wa, rg_ba, rg_wx, rg_bx, rg_lam, na_rpb,
           w_out, norm2_g, w_ffn_in, w_ffn_out, final_g):
    rg = [_rglru_weights(conv_w[l], conv_b[l], rg_wa[l], rg_ba[l], rg_wx[l], rg_bx[l], rg_lam[l])
          for l in range(DEPTH)]
    w = {
        "norm1_g": norm1_g, "norm2_g": norm2_g, "final_g": final_g,
        "w_in": w_in.astype(BF16), "w_out": w_out.astype(BF16),
        "w_ffn_in": w_ffn_in.astype(BF16), "w_ffn_out": w_ffn_out.astype(BF16),
        "rg_wg": [r[0] for r in rg], "rg_pk": [r[1] for r in rg],
        "na_tbl": [_na_bias_tables(na_rpb[l]) for l in range(DEPTH)],
    }
    return _encode(x_prompt, w), _encode(x_sample, w)
```

```python
import functools

import numpy as np
import jax
import jax.numpy as jnp
from jax import lax
from jax.experimental import pallas as pl
from jax.experimental.pallas import tpu as pltpu

D_MODEL = 2048
DEPTH = 2
HEAD_DIM = 64
A_PATTERNS = ((128, 1), (512, 4), (2048, 16))
A_HEADS_PER_GROUP = 4
A_HEADS = A_HEADS_PER_GROUP * len(A_PATTERNS)
A_WIDTH = A_HEADS * HEAD_DIM
A_OUT = A_HEADS_PER_GROUP * HEAD_DIM
B_WIDTH = 512
B_BLOCKS = 8
CONV_W = 4
RG_C = 8.0
C_HEADS = 12
C_WIDTH = C_HEADS * HEAD_DIM
GRID_W = 64
NA_KH = 8
NA_KW = 16
D_FF = 5632
D_IN = 3 * A_WIDTH + 2 * B_WIDTH + 3 * C_WIDTH
D_MIX_OUT = A_OUT + B_WIDTH + C_WIDTH
NORM_EPS = 1e-6

F32 = jnp.float32
BF16 = jnp.bfloat16

HEAD_GROUP_LANES = 4 * HEAD_DIM
SCALE = HEAD_DIM ** -0.5
NEG = -1e30

COL_QA, COL_KA, COL_VA = 0, 3, 6
COL_GB, COL_XB = 9, 11
COL_QC, COL_KC, COL_VC = 13, 16, 19

VMEM_LIMIT = 56 * 1024 * 1024


def _params(sem):
    return pltpu.CompilerParams(dimension_semantics=sem, vmem_limit_bytes=VMEM_LIMIT)


def _rmsnorm_rows(x, g):
    ms = jnp.mean(x * x, axis=-1, keepdims=True)
    return x * lax.rsqrt(ms + NORM_EPS) * g


def _norm_proj_kernel(x_ref, g_ref, w_ref, o_ref, xn_ref):
    @pl.when(pl.program_id(1) == 0)
    def _():
        xn_ref[...] = _rmsnorm_rows(x_ref[...], g_ref[...]).astype(BF16)

    o_ref[...] = jnp.dot(xn_ref[...], w_ref[...], preferred_element_type=F32).astype(o_ref.dtype)


def _norm_proj(x, g, w, tm, tn):
    m, d = x.shape
    n = w.shape[1]
    return pl.pallas_call(
        _norm_proj_kernel,
        grid=(m // tm, n // tn),
        in_specs=[
            pl.BlockSpec((tm, d), lambda i, j: (i, 0)),
            pl.BlockSpec((1, d), lambda i, j: (0, 0)),
            pl.BlockSpec((d, tn), lambda i, j: (0, j)),
        ],
        out_specs=pl.BlockSpec((tm, tn), lambda i, j: (i, j)),
        out_shape=jax.ShapeDtypeStruct((m, n), BF16),
        scratch_shapes=[pltpu.VMEM((tm, d), BF16)],
        compiler_params=_params(("parallel", "arbitrary")),
        name="norm_proj",
    )(x, g.reshape(1, d), w)


def _norm_swiglu_kernel(x_ref, g_ref, wg_ref, wu_ref, o_ref, xn_ref):
    @pl.when(pl.program_id(1) == 0)
    def _():
        xn_ref[...] = _rmsnorm_rows(x_ref[...], g_ref[...]).astype(BF16)

    xn = xn_ref[...]
    gate = jnp.dot(xn, wg_ref[...], preferred_element_type=F32)
    up = jnp.dot(xn, wu_ref[...], preferred_element_type=F32)
    o_ref[...] = (gate * jax.nn.sigmoid(gate) * up).astype(o_ref.dtype)


def _norm_swiglu(x, g, w, tm, tn):
    m, d = x.shape
    f = w.shape[1] // 2
    nf = f // tn
    return pl.pallas_call(
        _norm_swiglu_kernel,
        grid=(m // tm, nf),
        in_specs=[
            pl.BlockSpec((tm, d), lambda i, j: (i, 0)),
            pl.BlockSpec((1, d), lambda i, j: (0, 0)),
            pl.BlockSpec((d, tn), lambda i, j: (0, j)),
            pl.BlockSpec((d, tn), lambda i, j: (0, j + nf)),
        ],
        out_specs=pl.BlockSpec((tm, tn), lambda i, j: (i, j)),
        out_shape=jax.ShapeDtypeStruct((m, f), BF16),
        scratch_shapes=[pltpu.VMEM((tm, d), BF16)],
        compiler_params=_params(("parallel", "arbitrary")),
        name="norm_swiglu",
    )(x, g.reshape(1, d), w, w)


def _mix_out_kernel(res_ref, a_ref, b_ref, c_ref, w_ref, o_ref):
    acc = res_ref[...]
    acc = acc + jnp.dot(a_ref[...], w_ref[0:A_OUT, :], preferred_element_type=F32)
    acc = acc + jnp.dot(b_ref[...], w_ref[A_OUT:A_OUT + B_WIDTH, :], preferred_element_type=F32)
    acc = acc + jnp.dot(c_ref[...], w_ref[A_OUT + B_WIDTH:, :], preferred_element_type=F32)
    o_ref[...] = acc


def _mix_out(res, o_a, o_b, o_c, w, tm, tn):
    m, d = res.shape
    return pl.pallas_call(
        _mix_out_kernel,
        grid=(m // tm, d // tn),
        in_specs=[
            pl.BlockSpec((tm, tn), lambda i, j: (i, j)),
            pl.BlockSpec((tm, A_OUT), lambda i, j: (i, 0)),
            pl.BlockSpec((tm, B_WIDTH), lambda i, j: (i, 0)),
            pl.BlockSpec((tm, C_WIDTH), lambda i, j: (i, 0)),
            pl.BlockSpec((D_MIX_OUT, tn), lambda i, j: (0, j)),
        ],
        out_specs=pl.BlockSpec((tm, tn), lambda i, j: (i, j)),
        out_shape=jax.ShapeDtypeStruct((m, d), F32),
        compiler_params=_params(("parallel", "arbitrary")),
        name="mix_out",
    )(res, o_a, o_b, o_c, w)


def _ffn_out_kernel(res_ref, a_ref, w_ref, o_ref):
    o_ref[...] = res_ref[...] + jnp.dot(a_ref[...], w_ref[...], preferred_element_type=F32)


def _ffn_out(res, act, w, tm, tn):
    m, d = res.shape
    f = act.shape[1]
    return pl.pallas_call(
        _ffn_out_kernel,
        grid=(m // tm, d // tn),
        in_specs=[
            pl.BlockSpec((tm, tn), lambda i, j: (i, j)),
            pl.BlockSpec((tm, f), lambda i, j: (i, 0)),
            pl.BlockSpec((f, tn), lambda i, j: (0, j)),
        ],
        out_specs=pl.BlockSpec((tm, tn), lambda i, j: (i, j)),
        out_shape=jax.ShapeDtypeStruct((m, d), F32),
        compiler_params=_params(("parallel", "arbitrary")),
        name="ffn_out",
    )(res, act, w)


def _final_norm_kernel(x_ref, g_ref, o_ref):
    o_ref[...] = _rmsnorm_rows(x_ref[...], g_ref[...])


def _final_norm(x, g, tm):
    m, d = x.shape
    return pl.pallas_call(
        _final_norm_kernel,
        grid=(m // tm,),
        in_specs=[pl.BlockSpec((tm, d), lambda i: (i, 0)), pl.BlockSpec((1, d), lambda i: (0, 0))],
        out_specs=pl.BlockSpec((tm, d), lambda i: (i, 0)),
        out_shape=jax.ShapeDtypeStruct((m, d), F32),
        compiler_params=_params(("parallel",)),
        name="final_norm",
    )(x, g.reshape(1, d))


def _lane_head(rows):
    return lax.shift_right_logical(lax.broadcasted_iota(jnp.int32, (rows, HEAD_GROUP_LANES), 1), 6)


def _stack_heads(q, lane_head):
    qf = q.astype(F32) * SCALE
    return jnp.concatenate(
        [jnp.where(lane_head == h, qf, 0.0) for h in range(A_HEADS_PER_GROUP)], axis=0
    ).astype(BF16)


def _alibi_slope(h):
    return 2.0 ** (-8.0 * (h + 1) / A_HEADS)


LANES = 128
A_UNROLL = 4
A_COPY_ROWS = 256
A_CASES = 3


def _lane_halves(x):
    return [x[:, lh * LANES:(lh + 1) * LANES] for lh in range(HEAD_GROUP_LANES // LANES)]


def _dilated_group(gi, win, dil, t, q_ref, k_ref, v_ref, qd_ref, kd_ref, vd_ref, stage_ref, bias_ref,
                   m_ref, num_ref, den_ref):
    n = t // dil
    half = (win // 2) // dil
    tq = min(2 * half, n)
    nk = min(4 * half, n)
    tiles_per = n // tq
    n_halves = HEAD_GROUP_LANES // LANES

    if dil > 1:
        cr = min(A_COPY_ROWS, n)
        chunks_per = n // cr
        for src, dst in ((q_ref, qd_ref), (k_ref, kd_ref), (v_ref, vd_ref)):
            def stage(c, carry, src=src):
                c0 = pl.multiple_of(c * A_COPY_ROWS, A_COPY_ROWS)
                rows = _lane_halves(src[pl.ds(c0, A_COPY_ROWS), :].astype(F32))
                for lh in range(n_halves):
                    stage_ref[lh, pl.ds(c0, A_COPY_ROWS), :] = rows[lh]
                return carry

            lax.fori_loop(0, t // A_COPY_ROWS, stage, 0)

            def gather(c, carry, dst=dst):
                r = c // chunks_per
                u0 = pl.multiple_of((c % chunks_per) * cr, cr)
                rows = jnp.concatenate(
                    [stage_ref[lh, pl.ds(r + u0 * dil, cr, stride=dil), :] for lh in range(n_halves)], axis=1)
                dst[pl.ds(pl.multiple_of(r * n + u0, cr), cr), :] = rows.astype(BF16)
                return carry

            lax.fori_loop(0, dil * chunks_per, gather, 0)
        q_src, k_src, v_src = qd_ref, kd_ref, vd_ref
    else:
        q_src, k_src, v_src = q_ref, k_ref, v_ref

    row_iota = lax.broadcasted_iota(jnp.int32, (tq, nk), 0)
    col_iota = lax.broadcasted_iota(jnp.int32, (tq, nk), 1)
    for case in range(A_CASES):
        rel = jnp.abs(col_iota - case * half - row_iota)
        relf = rel.astype(F32)
        for h in range(A_HEADS_PER_GROUP):
            coef = _alibi_slope(gi * A_HEADS_PER_GROUP + h) * dil
            bias_ref[case * A_HEADS_PER_GROUP + h, 0:tq, 0:nk] = jnp.where(rel <= half, -coef * relf, NEG)

    lane_head = _lane_head(tq)

    def body(tile, carry):
        r = tile // tiles_per
        i = tile % tiles_per
        q0 = pl.multiple_of(i * tq, tq)
        start = pl.multiple_of(jnp.clip(q0 - half, 0, n - nk), half) if n > nk else 0
        case = (q0 - start) // half
        base = pl.multiple_of(r * n, tq)
        qs = _stack_heads(q_src[pl.ds(base + q0, tq), :], lane_head)
        kw = k_src[pl.ds(pl.multiple_of(base + start, 16), nk), :]
        vw = v_src[pl.ds(pl.multiple_of(base + start, 16), nk), :]
        s = lax.dot_general(qs, kw, (((1,), (1,)), ((), ())), preferred_element_type=F32)
        mx = jnp.zeros((tq, HEAD_GROUP_LANES), F32)
        den = jnp.zeros((tq, HEAD_GROUP_LANES), F32)
        num = jnp.zeros((tq, HEAD_GROUP_LANES), F32)
        for h in range(A_HEADS_PER_GROUP):
            sh = s[h * tq:(h + 1) * tq] + bias_ref[case * A_HEADS_PER_GROUP + h, 0:tq, 0:nk]
            m = jnp.max(sh, axis=-1, keepdims=True)
            p = jnp.exp(sh - m)
            l = jnp.sum(p, axis=-1, keepdims=True)
            pv = jnp.dot(p.astype(BF16), vw, preferred_element_type=F32)
            sel = lane_head == h
            num = jnp.where(sel, pv, num)
            den = jnp.where(sel, l, den)
            mx = jnp.where(sel, m, mx)
        for lh, (mx_h, num_h, den_h) in enumerate(zip(_lane_halves(mx), _lane_halves(num), _lane_halves(den))):
            if gi == 0:
                idx = pl.ds(q0, tq)
                m_ref[lh, idx, :] = mx_h
                num_ref[lh, idx, :] = num_h
                den_ref[lh, idx, :] = den_h
            else:
                idx = pl.ds(q0 * dil + r, tq, stride=dil)
                m_old = m_ref[lh, idx, :]
                m_new = jnp.maximum(m_old, mx_h)
                s_old = jnp.exp(m_old - m_new)
                s_new = jnp.exp(mx_h - m_new)
                num_ref[lh, idx, :] = num_ref[lh, idx, :] * s_old + num_h * s_new
                den_ref[lh, idx, :] = den_ref[lh, idx, :] * s_old + den_h * s_new
                m_ref[lh, idx, :] = m_new
        return carry

    lax.fori_loop(0, dil * tiles_per, body, 0, unroll=A_UNROLL)


def _dilated_kernel(q_ref, k_ref, v_ref, o_ref, qd_ref, kd_ref, vd_ref, stage_ref, bias_ref, m_ref, num_ref,
                    den_ref, *, t):
    g = pl.program_id(1)
    for gi, (win, dil) in enumerate(A_PATTERNS):
        @pl.when(g == gi)
        def _(gi=gi, win=win, dil=dil):
            _dilated_group(gi, win, dil, t, q_ref, k_ref, v_ref, qd_ref, kd_ref, vd_ref, stage_ref, bias_ref,
                           m_ref, num_ref, den_ref)

    @pl.when(g == len(A_PATTERNS) - 1)
    def _():
        def emit(c, carry):
            idx = pl.ds(pl.multiple_of(c * A_COPY_ROWS, A_COPY_ROWS), A_COPY_ROWS)
            o_ref[idx, :] = jnp.concatenate(
                [num_ref[lh, idx, :] / den_ref[lh, idx, :] for lh in range(HEAD_GROUP_LANES // LANES)],
                axis=1).astype(o_ref.dtype)
            return carry

        lax.fori_loop(0, t // A_COPY_ROWS, emit, 0)


def _mixer_dilated(proj3):
    bsz, t, _ = proj3.shape
    halves = {(win // 2) // dil for win, dil in A_PATTERNS}
    assert len(halves) == 1, "bias table scratch is sized for one band half-width"
    half = halves.pop()

    def col_spec(off):
        return pl.BlockSpec((None, t, HEAD_GROUP_LANES), lambda b, g: (b, 0, off + g))

    seq_bf16 = pltpu.VMEM((t, HEAD_GROUP_LANES), BF16)
    seq_halves = pltpu.VMEM((HEAD_GROUP_LANES // LANES, t, LANES), F32)
    return pl.pallas_call(
        functools.partial(_dilated_kernel, t=t),
        grid=(bsz, len(A_PATTERNS)),
        in_specs=[col_spec(COL_QA), col_spec(COL_KA), col_spec(COL_VA)],
        out_specs=pl.BlockSpec((None, t, A_OUT), lambda b, g: (b, 0, 0)),
        out_shape=jax.ShapeDtypeStruct((bsz, t, A_OUT), BF16),
        scratch_shapes=[seq_bf16, seq_bf16, seq_bf16, seq_halves,
                        pltpu.VMEM((A_CASES * A_HEADS_PER_GROUP, 2 * half, 4 * half), F32),
                        seq_halves, seq_halves, seq_halves],
        compiler_params=_params(("parallel", "arbitrary")),
        name="dilated_attn",
    )(proj3, proj3, proj3)


B_TILE = 256
B_CHUNK = 256
B_PAD = 8
SUBLANES = 8


def _scan8(a, b, reverse):
    row = lax.broadcasted_iota(jnp.int32, a.shape, 0)
    for sh in (1, 2, 4):
        if reverse:
            a_s = pltpu.roll(a, SUBLANES - sh, 0)
            b_s = pltpu.roll(b, SUBLANES - sh, 0)
            ok = row < SUBLANES - sh
        else:
            a_s = pltpu.roll(a, sh, 0)
            b_s = pltpu.roll(b, sh, 0)
            ok = row >= sh
        a_s = jnp.where(ok, a_s, 1.0)
        b_s = jnp.where(ok, b_s, 0.0)
        b = a * b_s + b
        a = a * a_s
    return a, b


def _softplus(z):
    return jnp.maximum(z, 0.0) + jnp.log1p(jnp.exp(-jnp.abs(z)))


def _rglru_kernel(gb_ref, xb_ref, wg_ref, pk_ref, o_ref, xs_ref, af_ref, bf_ref, ab_ref, bb_ref, *, t):
    ct = B_TILE
    pk = pk_ref[...]
    conv_w = [pk[j:j + 1, :] for j in range(CONV_W)]
    conv_b = pk[4:5, :]
    bias = [pk[5 + j:6 + j, :] for j in range(4)]
    sp = [_softplus(-pk[9 + j:10 + j, :]) for j in range(2)]

    xs_ref[0:B_PAD, :] = jnp.zeros((B_PAD, ct), F32)
    xs_ref[B_PAD + t:2 * B_PAD + t, :] = jnp.zeros((B_PAD, ct), F32)

    def fill(c, carry):
        c0 = pl.multiple_of(c * B_CHUNK, B_CHUNK)
        xs_ref[pl.ds(c0 + B_PAD, B_CHUNK), :] = xb_ref[pl.ds(c0, B_CHUNK), :].astype(F32)
        return carry

    lax.fori_loop(0, t // B_CHUNK, fill, 0)

    def gates(c, carry):
        c0 = pl.multiple_of(c * B_CHUNK, B_CHUNK)
        slab = xs_ref[pl.ds(c0, B_CHUNK + 2 * B_PAD), :]
        left = CONV_W // 2
        xc = conv_b + conv_w[0] * slab[B_PAD - left:B_PAD - left + B_CHUNK]
        for j in range(1, CONV_W):
            xc = xc + conv_w[j] * slab[B_PAD - left + j:B_PAD - left + j + B_CHUNK]
        g = jnp.dot(xc.astype(BF16), wg_ref[...], preferred_element_type=F32)
        for d, (a_ref, b_ref) in enumerate(((af_ref, bf_ref), (ab_ref, bb_ref))):
            r = jax.nn.sigmoid(g[:, (2 * d) * ct:(2 * d + 1) * ct] + bias[2 * d])
            i = jax.nn.sigmoid(g[:, (2 * d + 1) * ct:(2 * d + 2) * ct] + bias[2 * d + 1])
            log_a = -RG_C * r * sp[d]
            a = jnp.exp(log_a)
            one_minus_a2 = -jnp.tanh(log_a) * (a * a + 1.0)
            a_ref[pl.ds(c0, B_CHUNK), :] = a
            b_ref[pl.ds(c0, B_CHUNK), :] = jnp.sqrt(one_minus_a2) * (i * xc)
        return carry

    lax.fori_loop(0, t // B_CHUNK, gates, 0)

    def scan(k, carry):
        cf, cb = carry
        rf = pl.multiple_of(k * SUBLANES, SUBLANES)
        rb = pl.multiple_of(t - SUBLANES - k * SUBLANES, SUBLANES)
        a, b = _scan8(af_ref[pl.ds(rf, SUBLANES), :], bf_ref[pl.ds(rf, SUBLANES), :], False)
        hf = a * cf + b
        bf_ref[pl.ds(rf, SUBLANES), :] = hf
        a, b = _scan8(ab_ref[pl.ds(rb, SUBLANES), :], bb_ref[pl.ds(rb, SUBLANES), :], True)
        hb = a * cb + b
        bb_ref[pl.ds(rb, SUBLANES), :] = hb
        return hf[SUBLANES - 1:SUBLANES, :], hb[0:1, :]

    zero = jnp.zeros((1, ct), F32)
    lax.fori_loop(0, t // SUBLANES, scan, (zero, zero))

    def emit(c, carry):
        c0 = pl.multiple_of(c * B_CHUNK, B_CHUNK)
        h = bf_ref[pl.ds(c0, B_CHUNK), :] + bb_ref[pl.ds(c0, B_CHUNK), :]
        gate = gb_ref[pl.ds(c0, B_CHUNK), :].astype(F32)
        o_ref[pl.ds(c0, B_CHUNK), :] = (jax.nn.gelu(gate) * h).astype(o_ref.dtype)
        return carry

    lax.fori_loop(0, t // B_CHUNK, emit, 0)


def _rglru_weights(conv_w, conv_b, wa, ba, wx, bx, lam):
    eye = jnp.eye(B_BLOCKS, dtype=F32)

    def dense(w):
        return jnp.einsum("ncd,nm->ncmd", w, eye).reshape(B_WIDTH, B_WIDTH)

    mats = [dense(wa[0]), dense(wx[0]), dense(wa[1]), dense(wx[1])]
    nt = B_WIDTH // B_TILE
    wg = jnp.stack([
        jnp.concatenate([m[c * B_TILE:(c + 1) * B_TILE, c * B_TILE:(c + 1) * B_TILE] for m in mats], axis=1)
        for c in range(nt)
    ]).astype(BF16)
    rows = [conv_w[j] for j in range(CONV_W)] + [conv_b, ba[0], bx[0], ba[1], bx[1], lam[0], lam[1]]
    rows += [jnp.zeros_like(conv_b)] * (16 - len(rows))
    pk = jnp.stack(rows).reshape(16, nt, B_TILE).transpose(1, 0, 2)
    return wg, pk


def _mixer_rglru(proj3, wg, pk):
    bsz, t, _ = proj3.shape
    nt = B_WIDTH // B_TILE
    return pl.pallas_call(
        functools.partial(_rglru_kernel, t=t),
        grid=(bsz, nt),
        in_specs=[
            pl.BlockSpec((None, t, B_TILE), lambda b, c: (b, 0, COL_GB + c)),
            pl.BlockSpec((None, t, B_TILE), lambda b, c: (b, 0, COL_XB + c)),
            pl.BlockSpec((None, B_TILE, 4 * B_TILE), lambda b, c: (c, 0, 0)),
            pl.BlockSpec((None, 16, B_TILE), lambda b, c: (c, 0, 0)),
        ],
        out_specs=pl.BlockSpec((None, t, B_TILE), lambda b, c: (b, 0, c)),
        out_shape=jax.ShapeDtypeStruct((bsz, t, B_WIDTH), BF16),
        scratch_shapes=[pltpu.VMEM((t + 2 * B_PAD, B_TILE), F32)] + [pltpu.VMEM((t, B_TILE), F32)] * 4,
        compiler_params=_params(("parallel", "parallel")),
        name="rglru",
    )(proj3, proj3, wg, pk)


NA_KEYS = NA_KH * GRID_W
NA_UNROLL = 4


def _na_bias_tables(rpb):
    c = np.arange(GRID_W)
    cs = np.clip(c - NA_KW // 2, 0, GRID_W - NA_KW)
    kc = np.arange(GRID_W)
    inwin = (kc[None, :] >= cs[:, None]) & (kc[None, :] < cs[:, None] + NA_KW)
    off = np.clip(kc[None, :] - c[:, None] + NA_KW - 1, 0, 2 * NA_KW - 2)
    rowidx = np.arange(NA_KH)[:, None] + np.arange(NA_KH)[None, :]
    tb = rpb.astype(F32)[:, rowidx][:, :, :, off]
    tb = jnp.where(inwin[None, None, None], tb, NEG)
    tb = tb.transpose(0, 1, 3, 2, 4).reshape(C_HEADS // 4, 4, NA_KH, GRID_W, NA_KEYS)
    return tb.transpose(0, 2, 1, 3, 4).reshape(C_HEADS // 4, NA_KH, 4 * GRID_W, NA_KEYS)


def _na_kernel(q_ref, k_ref, v_ref, tbl_ref, o_ref, *, rows):
    lane_head = _lane_head(GRID_W)

    def body(r, carry):
        rs = jnp.clip(r - NA_KH // 2, 0, rows - NA_KH)
        dr0 = rs - r + NA_KH - 1
        q0 = pl.multiple_of(r * GRID_W, GRID_W)
        k0 = pl.multiple_of(rs * GRID_W, GRID_W)
        qs = _stack_heads(q_ref[pl.ds(q0, GRID_W), :], lane_head)
        kw = k_ref[pl.ds(k0, NA_KEYS), :]
        vw = v_ref[pl.ds(k0, NA_KEYS), :]
        s = lax.dot_general(qs, kw, (((1,), (1,)), ((), ())), preferred_element_type=F32)
        s = s + tbl_ref[dr0]
        m = jnp.max(s, axis=-1, keepdims=True)
        p = jnp.exp(s - m)
        l = jnp.sum(p, axis=-1, keepdims=True)
        pv = jnp.dot(p.astype(BF16), vw, preferred_element_type=F32) / l
        o = pv[0:GRID_W]
        for h in range(1, 4):
            o = jnp.where(lane_head == h, pv[h * GRID_W:(h + 1) * GRID_W], o)
        o_ref[pl.ds(q0, GRID_W), :] = o.astype(o_ref.dtype)
        return carry

    lax.fori_loop(0, rows, body, 0, unroll=NA_UNROLL)


def _mixer_na(proj3, tbl):
    bsz, t, _ = proj3.shape
    rows = t // GRID_W
    nhg = C_HEADS // 4

    def col_spec(off):
        return pl.BlockSpec((None, t, HEAD_GROUP_LANES), lambda hg, b: (b, 0, off + hg))

    return pl.pallas_call(
        functools.partial(_na_kernel, rows=rows),
        grid=(nhg, bsz),
        in_specs=[col_spec(COL_QC), col_spec(COL_KC), col_spec(COL_VC),
                  pl.BlockSpec((None, NA_KH, 4 * GRID_W, NA_KEYS), lambda hg, b: (hg, 0, 0, 0))],
        out_specs=pl.BlockSpec((None, t, HEAD_GROUP_LANES), lambda hg, b: (b, 0, hg)),
        out_shape=jax.ShapeDtypeStruct((bsz, t, C_WIDTH), BF16),
        compiler_params=_params(("parallel", "parallel")),
        name="na_attn",
    )(proj3, proj3, proj3, tbl)


def _tile(m, want):
    return want if m % want == 0 else m


def _encode(x, w):
    bsz, t, d = x.shape
    m = bsz * t
    tm = _tile(m, 1024)
    xf = x.reshape(m, d)
    for l in range(DEPTH):
        proj = _norm_proj(xf, w["norm1_g"][l], w["w_in"][l], tm, 512)
        proj3 = proj.reshape(bsz, t, D_IN)
        o_a = _mixer_dilated(proj3)
        o_b = _mixer_rglru(proj3, w["rg_wg"][l], w["rg_pk"][l])
        o_c = _mixer_na(proj3, w["na_tbl"][l])
        xf = _mix_out(xf, o_a.reshape(m, A_OUT), o_b.reshape(m, B_WIDTH), o_c.reshape(m, C_WIDTH),
                      w["w_out"][l], tm, 512)
        act = _norm_swiglu(xf, w["norm2_g"][l], w["w_ffn_in"][l], tm, 512)
        xf = _ffn_out(xf, act, w["w_ffn_out"][l], _tile(m, 512), 512)
    return _final_norm(xf, w["final_g"], _tile(m, 512)).reshape(bsz, t, d)


def kernel(x_prompt, x_sample, norm1_g, w_in, conv_w, conv_b, rg_wa, rg_ba, rg_wx, rg_bx, rg_lam, na_rpb,
           w_out, norm2_g, w_ffn_in, w_ffn_out, final_g):
    rg = [_rglru_weights(conv_w[l], conv_b[l], rg_wa[l], rg_ba[l], rg_wx[l], rg_bx[l], rg_lam[l])
          for l in range(DEPTH)]
    w = {
        "norm1_g": norm1_g, "norm2_g": norm2_g, "final_g": final_g,
        "w_in": w_in.astype(BF16), "w_out": w_out.astype(BF16),
        "w_ffn_in": w_ffn_in.astype(BF16), "w_ffn_out": w_ffn_out.astype(BF16),
        "rg_wg": [r[0] for r in rg], "rg_pk": [r[1] for r in rg],
        "na_tbl": [_na_bias_tables(na_rpb[l]) for l in range(DEPTH)],
    }
    return _encode(x_prompt, w), _encode(x_sample, w)
```

```python
import functools

import numpy as np
import jax
import jax.numpy as jnp
from jax import lax
from jax.experimental import pallas as pl
from jax.experimental.pallas import tpu as pltpu

D_MODEL = 2048
DEPTH = 2
HEAD_DIM = 64
A_PATTERNS = ((128, 1), (512, 4), (2048, 16))
A_HEADS_PER_GROUP = 4
A_HEADS = A_HEADS_PER_GROUP * len(A_PATTERNS)
A_WIDTH = A_HEADS * HEAD_DIM
A_OUT = A_HEADS_PER_GROUP * HEAD_DIM
B_WIDTH = 512
B_BLOCKS = 8
CONV_W = 4
RG_C = 8.0
C_HEADS = 12
C_WIDTH = C_HEADS * HEAD_DIM
GRID_W = 64
NA_KH = 8
NA_KW = 16
D_FF = 5632
D_IN = 3 * A_WIDTH + 2 * B_WIDTH + 3 * C_WIDTH
D_MIX_OUT = A_OUT + B_WIDTH + C_WIDTH
NORM_EPS = 1e-6

F32 = jnp.float32
BF16 = jnp.bfloat16

HEAD_GROUP_LANES = 4 * HEAD_DIM
SCALE = HEAD_DIM ** -0.5
NEG = -1e30

COL_QA, COL_KA, COL_VA = 0, 3, 6
COL_GB, COL_XB = 9, 11
COL_QC, COL_KC, COL_VC = 13, 16, 19

MIB = 1024 * 1024
VMEM_LIMIT = 56 * MIB
VMEM_INTERNAL = 5 * MIB


def _params(sem, vmem_bytes=VMEM_LIMIT):
    return pltpu.CompilerParams(dimension_semantics=sem, vmem_limit_bytes=vmem_bytes)


def _rmsnorm_rows(x, g):
    ms = jnp.mean(x * x, axis=-1, keepdims=True)
    return x * lax.rsqrt(ms + NORM_EPS) * g


def _norm_proj_kernel(x_ref, g_ref, w_ref, o_ref, xn_ref):
    @pl.when(pl.program_id(1) == 0)
    def _():
        xn_ref[...] = _rmsnorm_rows(x_ref[...], g_ref[...]).astype(BF16)

    o_ref[...] = jnp.dot(xn_ref[...], w_ref[...], preferred_element_type=F32).astype(o_ref.dtype)


def _norm_proj(x, g, w, tm, tn):
    m, d = x.shape
    n = w.shape[1]
    blocks = 2 * (tm * d * 4 + d * tn * 2 + tm * tn * 2) + tm * d * 2
    return pl.pallas_call(
        _norm_proj_kernel,
        grid=(m // tm, n // tn),
        in_specs=[
            pl.BlockSpec((tm, d), lambda i, j: (i, 0)),
            pl.BlockSpec((1, d), lambda i, j: (0, 0)),
            pl.BlockSpec((d, tn), lambda i, j: (0, j)),
        ],
        out_specs=pl.BlockSpec((tm, tn), lambda i, j: (i, j)),
        out_shape=jax.ShapeDtypeStruct((m, n), BF16),
        scratch_shapes=[pltpu.VMEM((tm, d), BF16)],
        compiler_params=_params(("parallel", "arbitrary"), blocks + VMEM_INTERNAL),
        name="norm_proj",
    )(x, g.reshape(1, d), w)


SWIGLU_ROWS = 256


def _swiglu_kernel(xn_ref, wg_ref, wu_ref, o_ref):
    for r0 in range(0, xn_ref.shape[0], SWIGLU_ROWS):
        rows = slice(r0, min(r0 + SWIGLU_ROWS, xn_ref.shape[0]))
        xn = xn_ref[rows, :]
        gate = jnp.dot(xn, wg_ref[...], preferred_element_type=F32)
        up = jnp.dot(xn, wu_ref[...], preferred_element_type=F32)
        o_ref[rows, :] = (gate * jax.nn.sigmoid(gate) * up).astype(o_ref.dtype)


def _swiglu(xn, w, tm, tn):
    m, d = xn.shape
    f = w.shape[1] // 2
    nf = f // tn
    return pl.pallas_call(
        _swiglu_kernel,
        grid=(m // tm, nf),
        in_specs=[
            pl.BlockSpec((tm, d), lambda i, j: (i, 0)),
            pl.BlockSpec((d, tn), lambda i, j: (0, j)),
            pl.BlockSpec((d, tn), lambda i, j: (0, j + nf)),
        ],
        out_specs=pl.BlockSpec((tm, tn), lambda i, j: (i, j)),
        out_shape=jax.ShapeDtypeStruct((m, f), BF16),
        compiler_params=_params(("parallel", "arbitrary")),
        name="swiglu",
    )(xn, w, w)


def _mix_out_kernel(res_ref, a_ref, b_ref, c_ref, w_ref, g_ref, o_ref, xn_ref):
    acc = res_ref[...]
    acc = acc + jnp.dot(a_ref[...], w_ref[0:A_OUT, :], preferred_element_type=F32)
    acc = acc + jnp.dot(b_ref[...], w_ref[A_OUT:A_OUT + B_WIDTH, :], preferred_element_type=F32)
    acc = acc + jnp.dot(c_ref[...], w_ref[A_OUT + B_WIDTH:, :], preferred_element_type=F32)
    o_ref[...] = acc
    xn_ref[...] = _rmsnorm_rows(acc, g_ref[...]).astype(BF16)


def _mix_out(res, o_a, o_b, o_c, w, g, tm):
    m, d = res.shape
    row = lambda width: pl.BlockSpec((tm, width), lambda i: (i, 0))
    return pl.pallas_call(
        _mix_out_kernel,
        grid=(m // tm,),
        in_specs=[row(d), row(A_OUT), row(B_WIDTH), row(C_WIDTH),
                  pl.BlockSpec((D_MIX_OUT, d), lambda i: (0, 0)),
                  pl.BlockSpec((1, d), lambda i: (0, 0))],
        out_specs=(row(d), row(d)),
        out_shape=(jax.ShapeDtypeStruct((m, d), F32), jax.ShapeDtypeStruct((m, d), BF16)),
        compiler_params=_params(("parallel",)),
        name="mix_out",
    )(res, o_a, o_b, o_c, w, g.reshape(1, d))


def _ffn_out_kernel(res_ref, a_ref, w_ref, o_ref):
    o_ref[...] = res_ref[...] + jnp.dot(a_ref[...], w_ref[...], preferred_element_type=F32)


def _ffn_out(res, act, w, tm, tn):
    m, d = res.shape
    f = act.shape[1]
    return pl.pallas_call(
        _ffn_out_kernel,
        grid=(m // tm, d // tn),
        in_specs=[
            pl.BlockSpec((tm, tn), lambda i, j: (i, j)),
            pl.BlockSpec((tm, f), lambda i, j: (i, 0)),
            pl.BlockSpec((f, tn), lambda i, j: (0, j)),
        ],
        out_specs=pl.BlockSpec((tm, tn), lambda i, j: (i, j)),
        out_shape=jax.ShapeDtypeStruct((m, d), F32),
        compiler_params=_params(("parallel", "arbitrary")),
        name="ffn_out",
    )(res, act, w)


def _ffn_out_norm_kernel(res_ref, a_ref, w_ref, g_ref, o_ref, *, tn):
    j = pl.program_id(1)
    cols = pl.ds(pl.multiple_of(j * tn, tn), tn)
    o_ref[:, cols] = res_ref[...] + jnp.dot(a_ref[...], w_ref[...], preferred_element_type=F32)

    @pl.when(j == pl.num_programs(1) - 1)
    def _():
        o_ref[...] = _rmsnorm_rows(o_ref[...], g_ref[...])


def _ffn_out_norm(res, act, w, g, tm, tn):
    m, d = res.shape
    f = act.shape[1]
    return pl.pallas_call(
        functools.partial(_ffn_out_norm_kernel, tn=tn),
        grid=(m // tm, d // tn),
        in_specs=[
            pl.BlockSpec((tm, tn), lambda i, j: (i, j)),
            pl.BlockSpec((tm, f), lambda i, j: (i, 0)),
            pl.BlockSpec((f, tn), lambda i, j: (0, j)),
            pl.BlockSpec((1, d), lambda i, j: (0, 0)),
        ],
        out_specs=pl.BlockSpec((tm, d), lambda i, j: (i, 0)),
        out_shape=jax.ShapeDtypeStruct((m, d), F32),
        compiler_params=_params(("parallel", "arbitrary")),
        name="ffn_out_norm",
    )(res, act, w, g.reshape(1, d))


def _lane_head(rows):
    return lax.shift_right_logical(lax.broadcasted_iota(jnp.int32, (rows, HEAD_GROUP_LANES), 1), 6)


def _stack_heads(q, lane_head):
    qf = q.astype(F32) * SCALE
    return jnp.concatenate(
        [jnp.where(lane_head == h, qf, 0.0) for h in range(A_HEADS_PER_GROUP)], axis=0
    ).astype(BF16)


def _alibi_slope(h):
    return 2.0 ** (-8.0 * (h + 1) / A_HEADS)


LANES = 128
BF16_ROWS = 16
A_UNROLL = 4
A_COPY_ROWS = 256
A_CASES = 3


def _lane_halves(x):
    return [x[:, lh * LANES:(lh + 1) * LANES] for lh in range(HEAD_GROUP_LANES // LANES)]


def _dilated_group(gi, win, dil, t, q_ref, k_ref, v_ref, qd_ref, kd_ref, vd_ref, stage_ref, bias_ref,
                   m_ref, num_ref, den_ref):
    n = t // dil
    half = (win // 2) // dil
    tq = min(2 * half, n)
    nk = min(4 * half, n)
    tiles_per = n // tq
    n_halves = HEAD_GROUP_LANES // LANES

    if dil > 1:
        cr = min(A_COPY_ROWS, n)
        chunks_per = n // cr
        for src, dst in ((q_ref, qd_ref), (k_ref, kd_ref), (v_ref, vd_ref)):
            def stage(c, carry, src=src):
                c0 = pl.multiple_of(c * A_COPY_ROWS, A_COPY_ROWS)
                rows = _lane_halves(src[pl.ds(c0, A_COPY_ROWS), :].astype(F32))
                for lh in range(n_halves):
                    stage_ref[lh, pl.ds(c0, A_COPY_ROWS), :] = rows[lh]
                return carry

            lax.fori_loop(0, t // A_COPY_ROWS, stage, 0)

            def gather(c, carry, dst=dst):
                r = c // chunks_per
                u0 = pl.multiple_of((c % chunks_per) * cr, cr)
                rows = jnp.concatenate(
                    [stage_ref[lh, pl.ds(r + u0 * dil, cr, stride=dil), :] for lh in range(n_halves)], axis=1)
                dst[pl.ds(pl.multiple_of(r * n + u0, cr), cr), :] = rows.astype(BF16)
                return carry

            lax.fori_loop(0, dil * chunks_per, gather, 0)
        q_src, k_src, v_src = qd_ref, kd_ref, vd_ref
    else:
        q_src, k_src, v_src = q_ref, k_ref, v_ref

    row_iota = lax.broadcasted_iota(jnp.int32, (tq, nk), 0)
    col_iota = lax.broadcasted_iota(jnp.int32, (tq, nk), 1)
    for case in range(A_CASES):
        rel = jnp.abs(col_iota - case * half - row_iota)
        relf = rel.astype(F32)
        for h in range(A_HEADS_PER_GROUP):
            coef = _alibi_slope(gi * A_HEADS_PER_GROUP + h) * dil
            bias_ref[case * A_HEADS_PER_GROUP + h, 0:tq, 0:nk] = jnp.where(rel <= half, -coef * relf, NEG)

    lane_head = _lane_head(tq)

    def body(tile, carry):
        r = tile // tiles_per
        i = tile % tiles_per
        q0 = pl.multiple_of(i * tq, tq)
        start = pl.multiple_of(jnp.clip(q0 - half, 0, n - nk), half) if n > nk else 0
        case = (q0 - start) // half
        base = pl.multiple_of(r * n, tq)
        qs = _stack_heads(q_src[pl.ds(base + q0, tq), :], lane_head)
        k0 = pl.multiple_of(base + start, BF16_ROWS)
        kw = k_src[pl.ds(k0, nk), :]
        vw = v_src[pl.ds(k0, nk), :]
        s = lax.dot_general(qs, kw, (((1,), (1,)), ((), ())), preferred_element_type=F32)
        mx = jnp.zeros((tq, HEAD_GROUP_LANES), F32)
        den = jnp.zeros((tq, HEAD_GROUP_LANES), F32)
        num = jnp.zeros((tq, HEAD_GROUP_LANES), F32)
        for h in range(A_HEADS_PER_GROUP):
            sh = s[h * tq:(h + 1) * tq] + bias_ref[case * A_HEADS_PER_GROUP + h, 0:tq, 0:nk]
            m = jnp.max(sh, axis=-1, keepdims=True)
            p = jnp.exp(sh - m)
            l = jnp.sum(p, axis=-1, keepdims=True)
            pv = jnp.dot(p.astype(BF16), vw, preferred_element_type=F32)
            sel = lane_head == h
            num = jnp.where(sel, pv, num)
            den = jnp.where(sel, l, den)
            mx = jnp.where(sel, m, mx)
        for lh, (mx_h, num_h, den_h) in enumerate(zip(_lane_halves(mx), _lane_halves(num), _lane_halves(den))):
            if gi == 0:
                idx = pl.ds(q0, tq)
                m_ref[lh, idx, :] = mx_h
                num_ref[lh, idx, :] = num_h
                den_ref[lh, idx, :] = den_h
            else:
                idx = pl.ds(q0 * dil + r, tq, stride=dil)
                m_old = m_ref[lh, idx, :]
                m_new = jnp.maximum(m_old, mx_h)
                s_old = jnp.exp(m_old - m_new)
                s_new = jnp.exp(mx_h - m_new)
                num_ref[lh, idx, :] = num_ref[lh, idx, :] * s_old + num_h * s_new
                den_ref[lh, idx, :] = den_ref[lh, idx, :] * s_old + den_h * s_new
                m_ref[lh, idx, :] = m_new
        return carry

    lax.fori_loop(0, dil * tiles_per, body, 0, unroll=A_UNROLL)


def _dilated_kernel(q_ref, k_ref, v_ref, o_ref, qd_ref, kd_ref, vd_ref, stage_ref, bias_ref, m_ref, num_ref,
                    den_ref, *, t):
    g = pl.program_id(1)
    for gi, (win, dil) in enumerate(A_PATTERNS):
        @pl.when(g == gi)
        def _(gi=gi, win=win, dil=dil):
            _dilated_group(gi, win, dil, t, q_ref, k_ref, v_ref, qd_ref, kd_ref, vd_ref, stage_ref, bias_ref,
                           m_ref, num_ref, den_ref)

    @pl.when(g == len(A_PATTERNS) - 1)
    def _():
        def emit(c, carry):
            idx = pl.ds(pl.multiple_of(c * A_COPY_ROWS, A_COPY_ROWS), A_COPY_ROWS)
            o_ref[idx, :] = jnp.concatenate(
                [num_ref[lh, idx, :] / den_ref[lh, idx, :] for lh in range(HEAD_GROUP_LANES // LANES)],
                axis=1).astype(o_ref.dtype)
            return carry

        lax.fori_loop(0, t // A_COPY_ROWS, emit, 0)


def _mixer_dilated(proj3):
    bsz, t, _ = proj3.shape
    halves = {(win // 2) // dil for win, dil in A_PATTERNS}
    assert len(halves) == 1, "bias table scratch is sized for one band half-width"
    half = halves.pop()

    def col_spec(off):
        return pl.BlockSpec((None, t, HEAD_GROUP_LANES), lambda b, g: (b, 0, off + g))

    seq_bf16 = pltpu.VMEM((t, HEAD_GROUP_LANES), BF16)
    seq_halves = pltpu.VMEM((HEAD_GROUP_LANES // LANES, t, LANES), F32)
    return pl.pallas_call(
        functools.partial(_dilated_kernel, t=t),
        grid=(bsz, len(A_PATTERNS)),
        in_specs=[col_spec(COL_QA), col_spec(COL_KA), col_spec(COL_VA)],
        out_specs=pl.BlockSpec((None, t, A_OUT), lambda b, g: (b, 0, 0)),
        out_shape=jax.ShapeDtypeStruct((bsz, t, A_OUT), BF16),
        scratch_shapes=[seq_bf16, seq_bf16, seq_bf16, seq_halves,
                        pltpu.VMEM((A_CASES * A_HEADS_PER_GROUP, 2 * half, 4 * half), F32),
                        seq_halves, seq_halves, seq_halves],
        compiler_params=_params(("parallel", "arbitrary")),
        name="dilated_attn",
    )(proj3, proj3, proj3)


B_TILE = 256
B_CHUNK = 256
B_PAD = 8
SUBLANES = 8


def _scan8(a, b, reverse):
    row = lax.broadcasted_iota(jnp.int32, a.shape, 0)
    for sh in (1, 2, 4):
        if reverse:
            a_s = pltpu.roll(a, SUBLANES - sh, 0)
            b_s = pltpu.roll(b, SUBLANES - sh, 0)
            ok = row < SUBLANES - sh
        else:
            a_s = pltpu.roll(a, sh, 0)
            b_s = pltpu.roll(b, sh, 0)
            ok = row >= sh
        a_s = jnp.where(ok, a_s, 1.0)
        b_s = jnp.where(ok, b_s, 0.0)
        b = a * b_s + b
        a = a * a_s
    return a, b


def _softplus(z):
    return jnp.maximum(z, 0.0) + jnp.log1p(jnp.exp(-jnp.abs(z)))


def _rglru_kernel(gb_ref, xb_ref, wg_ref, pk_ref, o_ref, xs_ref, af_ref, bf_ref, ab_ref, bb_ref, *, t):
    ct = B_TILE
    pk = pk_ref[...]
    conv_w = [pk[j:j + 1, :] for j in range(CONV_W)]
    conv_b = pk[4:5, :]
    bias = [pk[5 + j:6 + j, :] for j in range(4)]
    sp = [_softplus(-pk[9 + j:10 + j, :]) for j in range(2)]

    xs_ref[0:B_PAD, :] = jnp.zeros((B_PAD, ct), F32)
    xs_ref[B_PAD + t:2 * B_PAD + t, :] = jnp.zeros((B_PAD, ct), F32)

    def fill(c, carry):
        c0 = pl.multiple_of(c * B_CHUNK, B_CHUNK)
        xs_ref[pl.ds(c0 + B_PAD, B_CHUNK), :] = xb_ref[pl.ds(c0, B_CHUNK), :].astype(F32)
        return carry

    lax.fori_loop(0, t // B_CHUNK, fill, 0)

    def gates(c, carry):
        c0 = pl.multiple_of(c * B_CHUNK, B_CHUNK)
        slab = xs_ref[pl.ds(c0, B_CHUNK + 2 * B_PAD), :]
        left = CONV_W // 2
        xc = conv_b + conv_w[0] * slab[B_PAD - left:B_PAD - left + B_CHUNK]
        for j in range(1, CONV_W):
            xc = xc + conv_w[j] * slab[B_PAD - left + j:B_PAD - left + j + B_CHUNK]
        g = jnp.dot(xc.astype(BF16), wg_ref[...], preferred_element_type=F32)
        for d, (a_ref, b_ref) in enumerate(((af_ref, bf_ref), (ab_ref, bb_ref))):
            r = jax.nn.sigmoid(g[:, (2 * d) * ct:(2 * d + 1) * ct] + bias[2 * d])
            i = jax.nn.sigmoid(g[:, (2 * d + 1) * ct:(2 * d + 2) * ct] + bias[2 * d + 1])
            log_a = -RG_C * r * sp[d]
            a = jnp.exp(log_a)
            one_minus_a2 = -jnp.tanh(log_a) * (a * a + 1.0)
            a_ref[pl.ds(c0, B_CHUNK), :] = a
            b_ref[pl.ds(c0, B_CHUNK), :] = jnp.sqrt(one_minus_a2) * (i * xc)
        return carry

    lax.fori_loop(0, t // B_CHUNK, gates, 0)

    def scan(k, carry):
        cf, cb = carry
        rf = pl.multiple_of(k * SUBLANES, SUBLANES)
        rb = pl.multiple_of(t - SUBLANES - k * SUBLANES, SUBLANES)
        a, b = _scan8(af_ref[pl.ds(rf, SUBLANES), :], bf_ref[pl.ds(rf, SUBLANES), :], False)
        hf = a * cf + b
        bf_ref[pl.ds(rf, SUBLANES), :] = hf
        a, b = _scan8(ab_ref[pl.ds(rb, SUBLANES), :], bb_ref[pl.ds(rb, SUBLANES), :], True)
        hb = a * cb + b
        bb_ref[pl.ds(rb, SUBLANES), :] = hb
        return hf[SUBLANES - 1:SUBLANES, :], hb[0:1, :]

    zero = jnp.zeros((1, ct), F32)
    lax.fori_loop(0, t // SUBLANES, scan, (zero, zero))

    def emit(c, carry):
        c0 = pl.multiple_of(c * B_CHUNK, B_CHUNK)
        h = bf_ref[pl.ds(c0, B_CHUNK), :] + bb_ref[pl.ds(c0, B_CHUNK), :]
        gate = gb_ref[pl.ds(c0, B_CHUNK), :].astype(F32)
        o_ref[pl.ds(c0, B_CHUNK), :] = (jax.nn.gelu(gate) * h).astype(o_ref.dtype)
        return carry

    lax.fori_loop(0, t // B_CHUNK, emit, 0)


def _rglru_weights(conv_w, conv_b, wa, ba, wx, bx, lam):
    eye = jnp.eye(B_BLOCKS, dtype=F32)

    def dense(w):
        return jnp.einsum("ncd,nm->ncmd", w, eye).reshape(B_WIDTH, B_WIDTH)

    mats = [dense(wa[0]), dense(wx[0]), dense(wa[1]), dense(wx[1])]
    nt = B_WIDTH // B_TILE
    wg = jnp.stack([
        jnp.concatenate([m[c * B_TILE:(c + 1) * B_TILE, c * B_TILE:(c + 1) * B_TILE] for m in mats], axis=1)
        for c in range(nt)
    ]).astype(BF16)
    rows = [conv_w[j] for j in range(CONV_W)] + [conv_b, ba[0], bx[0], ba[1], bx[1], lam[0], lam[1]]
    rows += [jnp.zeros_like(conv_b)] * (16 - len(rows))
    pk = jnp.stack(rows).reshape(16, nt, B_TILE).transpose(1, 0, 2)
    return wg, pk


def _mixer_rglru(proj3, wg, pk):
    bsz, t, _ = proj3.shape
    nt = B_WIDTH // B_TILE
    return pl.pallas_call(
        functools.partial(_rglru_kernel, t=t),
        grid=(bsz, nt),
        in_specs=[
            pl.BlockSpec((None, t, B_TILE), lambda b, c: (b, 0, COL_GB + c)),
            pl.BlockSpec((None, t, B_TILE), lambda b, c: (b, 0, COL_XB + c)),
            pl.BlockSpec((None, B_TILE, 4 * B_TILE), lambda b, c: (c, 0, 0)),
            pl.BlockSpec((None, 16, B_TILE), lambda b, c: (c, 0, 0)),
        ],
        out_specs=pl.BlockSpec((None, t, B_TILE), lambda b, c: (b, 0, c)),
        out_shape=jax.ShapeDtypeStruct((bsz, t, B_WIDTH), BF16),
        scratch_shapes=[pltpu.VMEM((t + 2 * B_PAD, B_TILE), F32)] + [pltpu.VMEM((t, B_TILE), F32)] * 4,
        compiler_params=_params(("parallel", "parallel")),
        name="rglru",
    )(proj3, proj3, wg, pk)


NA_KEYS = NA_KH * GRID_W
NA_UNROLL = 4


def _na_bias_tables(rpb):
    c = np.arange(GRID_W)
    cs = np.clip(c - NA_KW // 2, 0, GRID_W - NA_KW)
    kc = np.arange(GRID_W)
    inwin = (kc[None, :] >= cs[:, None]) & (kc[None, :] < cs[:, None] + NA_KW)
    off = np.clip(kc[None, :] - c[:, None] + NA_KW - 1, 0, 2 * NA_KW - 2)
    tb = jnp.take(rpb.astype(F32), jnp.asarray(off), axis=2)
    tb = jnp.where(inwin[None, None], tb, NEG).transpose(0, 2, 1, 3)
    win = jnp.stack([tb[:, :, dr0:dr0 + NA_KH, :].reshape(C_HEADS, GRID_W, NA_KEYS) for dr0 in range(NA_KH)],
                    axis=1)
    win = win.reshape(C_HEADS // 4, 4, NA_KH, GRID_W, NA_KEYS).transpose(0, 2, 1, 3, 4)
    return win.reshape(C_HEADS // 4, NA_KH, 4 * GRID_W, NA_KEYS)


def _na_kernel(q_ref, k_ref, v_ref, tbl_ref, o_ref, *, rows):
    lane_head = _lane_head(GRID_W)

    def body(r, carry):
        rs = jnp.clip(r - NA_KH // 2, 0, rows - NA_KH)
        dr0 = rs - r + NA_KH - 1
        q0 = pl.multiple_of(r * GRID_W, GRID_W)
        k0 = pl.multiple_of(rs * GRID_W, GRID_W)
        qs = _stack_heads(q_ref[pl.ds(q0, GRID_W), :], lane_head)
        kw = k_ref[pl.ds(k0, NA_KEYS), :]
        vw = v_ref[pl.ds(k0, NA_KEYS), :]
        s = lax.dot_general(qs, kw, (((1,), (1,)), ((), ())), preferred_element_type=F32)
        s = s + tbl_ref[dr0]
        m = jnp.max(s, axis=-1, keepdims=True)
        p = jnp.exp(s - m)
        l = jnp.sum(p, axis=-1, keepdims=True)
        pv = jnp.dot(p.astype(BF16), vw, preferred_element_type=F32) / l
        o = pv[0:GRID_W]
        for h in range(1, 4):
            o = jnp.where(lane_head == h, pv[h * GRID_W:(h + 1) * GRID_W], o)
        o_ref[pl.ds(q0, GRID_W), :] = o.astype(o_ref.dtype)
        return carry

    lax.fori_loop(0, rows, body, 0, unroll=NA_UNROLL)


def _mixer_na(proj3, tbl):
    bsz, t, _ = proj3.shape
    rows = t // GRID_W
    nhg = C_HEADS // 4

    def col_spec(off):
        return pl.BlockSpec((None, t, HEAD_GROUP_LANES), lambda hg, b: (b, 0, off + hg))

    return pl.pallas_call(
        functools.partial(_na_kernel, rows=rows),
        grid=(nhg, bsz),
        in_specs=[col_spec(COL_QC), col_spec(COL_KC), col_spec(COL_VC),
                  pl.BlockSpec((None, NA_KH, 4 * GRID_W, NA_KEYS), lambda hg, b: (hg, 0, 0, 0))],
        out_specs=pl.BlockSpec((None, t, HEAD_GROUP_LANES), lambda hg, b: (b, 0, hg)),
        out_shape=jax.ShapeDtypeStruct((bsz, t, C_WIDTH), BF16),
        compiler_params=_params(("parallel", "parallel")),
        name="na_attn",
    )(proj3, proj3, proj3, tbl)


def _tile(m, want):
    return want if m % want == 0 else m


TILE_PROJ = (1024, D_IN // 2)
TILE_MIX_OUT_ROWS = 512
TILE_SWIGLU = (2048, 512)
TILE_FFN_OUT = (1024, 512)
TILE_FFN_OUT_NORM = (512, 512)


def _encode(x, w):
    bsz, t, d = x.shape
    m = bsz * t
    xf = x.reshape(m, d)
    for l in range(DEPTH):
        proj = _norm_proj(xf, w["norm1_g"][l], w["w_in"][l], _tile(m, TILE_PROJ[0]), TILE_PROJ[1])
        proj3 = proj.reshape(bsz, t, D_IN)
        o_a = _mixer_dilated(proj3)
        o_b = _mixer_rglru(proj3, w["rg_wg"][l], w["rg_pk"][l])
        o_c = _mixer_na(proj3, w["na_tbl"][l])
        xf, xn = _mix_out(xf, o_a.reshape(m, A_OUT), o_b.reshape(m, B_WIDTH), o_c.reshape(m, C_WIDTH),
                          w["w_out"][l], w["norm2_g"][l], _tile(m, TILE_MIX_OUT_ROWS))
        act = _swiglu(xn, w["w_ffn_in"][l], _tile(m, TILE_SWIGLU[0]), TILE_SWIGLU[1])
        if l < DEPTH - 1:
            xf = _ffn_out(xf, act, w["w_ffn_out"][l], _tile(m, TILE_FFN_OUT[0]), TILE_FFN_OUT[1])
        else:
            xf = _ffn_out_norm(xf, act, w["w_ffn_out"][l], w["final_g"], _tile(m, TILE_FFN_OUT_NORM[0]),
                               TILE_FFN_OUT_NORM[1])
    return xf.reshape(bsz, t, d)


def kernel(x_prompt, x_sample, norm1_g, w_in, conv_w, conv_b, rg_wa, rg_ba, rg_wx, rg_bx, rg_lam, na_rpb,
           w_out, norm2_g, w_ffn_in, w_ffn_out, final_g):
    rg = [_rglru_weights(conv_w[l], conv_b[l], rg_wa[l], rg_ba[l], rg_wx[l], rg_bx[l], rg_lam[l])
          for l in range(DEPTH)]
    w = {
        "norm1_g": norm1_g, "norm2_g": norm2_g, "final_g": final_g,
        "w_in": w_in.astype(BF16), "w_out": w_out.astype(BF16),
        "w_ffn_in": w_ffn_in.astype(BF16), "w_ffn_out": w_ffn_out.astype(BF16),
        "rg_wg": [r[0] for r in rg], "rg_pk": [r[1] for r in rg],
        "na_tbl": [_na_bias_tables(na_rpb[l]) for l in range(DEPTH)],
    }
    return _encode(x_prompt, w), _encode(x_sample, w)
```

```python
import functools

import numpy as np
import jax
import jax.numpy as jnp
from jax import lax
from jax.experimental import pallas as pl
from jax.experimental.pallas import tpu as pltpu

D_MODEL = 2048
DEPTH = 2
HEAD_DIM = 64
A_PATTERNS = ((128, 1), (512, 4), (2048, 16))
A_HEADS_PER_GROUP = 4
A_HEADS = A_HEADS_PER_GROUP * len(A_PATTERNS)
A_WIDTH = A_HEADS * HEAD_DIM
A_OUT = A_HEADS_PER_GROUP * HEAD_DIM
B_WIDTH = 512
B_BLOCKS = 8
CONV_W = 4
RG_C = 8.0
C_HEADS = 12
C_WIDTH = C_HEADS * HEAD_DIM
GRID_W = 64
NA_KH = 8
NA_KW = 16
D_FF = 5632
D_IN = 3 * A_WIDTH + 2 * B_WIDTH + 3 * C_WIDTH
D_MIX_OUT = A_OUT + B_WIDTH + C_WIDTH
NORM_EPS = 1e-6

F32 = jnp.float32
BF16 = jnp.bfloat16

HEAD_GROUP_LANES = 4 * HEAD_DIM
SCALE = HEAD_DIM ** -0.5
NEG = -1e30

COL_QA, COL_KA, COL_VA = 0, 3, 6
COL_GB, COL_XB = 9, 11
COL_QC, COL_KC, COL_VC = 13, 16, 19

MIB = 1024 * 1024
VMEM_LIMIT = 56 * MIB
VMEM_INTERNAL = 5 * MIB


def _params(sem, vmem_bytes=VMEM_LIMIT):
    return pltpu.CompilerParams(dimension_semantics=sem, vmem_limit_bytes=vmem_bytes)


def _rmsnorm_rows(x, g):
    ms = jnp.mean(x * x, axis=-1, keepdims=True)
    return x * lax.rsqrt(ms + NORM_EPS) * g


def _norm_proj_kernel(x_ref, g_ref, w_ref, o_ref, xn_ref):
    @pl.when(pl.program_id(1) == 0)
    def _():
        xn_ref[...] = _rmsnorm_rows(x_ref[...], g_ref[...]).astype(BF16)

    o_ref[...] = jnp.dot(xn_ref[...], w_ref[...], preferred_element_type=F32).astype(o_ref.dtype)


def _norm_proj(x, g, w, layer, tm, tn):
    m, d = x.shape
    n = w.shape[2]
    blocks = 2 * (tm * d * 4 + d * tn * 2 + tm * tn * 2) + tm * d * 2
    return pl.pallas_call(
        _norm_proj_kernel,
        grid=(m // tm, n // tn),
        in_specs=[
            pl.BlockSpec((tm, d), lambda i, j: (i, 0)),
            pl.BlockSpec((1, d), lambda i, j: (0, 0)),
            pl.BlockSpec((None, d, tn), lambda i, j: (layer, 0, j)),
        ],
        out_specs=pl.BlockSpec((tm, tn), lambda i, j: (i, j)),
        out_shape=jax.ShapeDtypeStruct((m, n), BF16),
        scratch_shapes=[pltpu.VMEM((tm, d), BF16)],
        compiler_params=_params(("parallel", "arbitrary"), blocks + VMEM_INTERNAL),
        name="norm_proj",
    )(x, g.reshape(1, d), w)


SWIGLU_ROWS = 256


def _swiglu_kernel(xn_ref, wg_ref, wu_ref, o_ref):
    for r0 in range(0, xn_ref.shape[0], SWIGLU_ROWS):
        rows = slice(r0, min(r0 + SWIGLU_ROWS, xn_ref.shape[0]))
        xn = xn_ref[rows, :]
        gate = jnp.dot(xn, wg_ref[...], preferred_element_type=F32)
        up = jnp.dot(xn, wu_ref[...], preferred_element_type=F32)
        o_ref[rows, :] = (gate * jax.nn.sigmoid(gate) * up).astype(o_ref.dtype)


def _swiglu(xn, w, layer, tm, tn):
    m, d = xn.shape
    f = w.shape[2] // 2
    nf = f // tn
    return pl.pallas_call(
        _swiglu_kernel,
        grid=(m // tm, nf),
        in_specs=[
            pl.BlockSpec((tm, d), lambda i, j: (i, 0)),
            pl.BlockSpec((None, d, tn), lambda i, j: (layer, 0, j)),
            pl.BlockSpec((None, d, tn), lambda i, j: (layer, 0, j + nf)),
        ],
        out_specs=pl.BlockSpec((tm, tn), lambda i, j: (i, j)),
        out_shape=jax.ShapeDtypeStruct((m, f), BF16),
        compiler_params=_params(("parallel", "arbitrary")),
        name="swiglu",
    )(xn, w, w)


def _mix_out_kernel(res_ref, a_ref, b_ref, c_ref, w_ref, g_ref, o_ref, xn_ref):
    acc = res_ref[...]
    acc = acc + jnp.dot(a_ref[...], w_ref[0:A_OUT, :], preferred_element_type=F32)
    acc = acc + jnp.dot(b_ref[...], w_ref[A_OUT:A_OUT + B_WIDTH, :], preferred_element_type=F32)
    acc = acc + jnp.dot(c_ref[...], w_ref[A_OUT + B_WIDTH:, :], preferred_element_type=F32)
    o_ref[...] = acc
    xn_ref[...] = _rmsnorm_rows(acc, g_ref[...]).astype(BF16)


def _mix_out(res, o_a, o_b, o_c, w, layer, g, tm):
    m, d = res.shape
    row = lambda width: pl.BlockSpec((tm, width), lambda i: (i, 0))
    return pl.pallas_call(
        _mix_out_kernel,
        grid=(m // tm,),
        in_specs=[row(d), row(A_OUT), row(B_WIDTH), row(C_WIDTH),
                  pl.BlockSpec((None, D_MIX_OUT, d), lambda i: (layer, 0, 0)),
                  pl.BlockSpec((1, d), lambda i: (0, 0))],
        out_specs=(row(d), row(d)),
        out_shape=(jax.ShapeDtypeStruct((m, d), F32), jax.ShapeDtypeStruct((m, d), BF16)),
        compiler_params=_params(("parallel",)),
        name="mix_out",
    )(res, o_a, o_b, o_c, w, g.reshape(1, d))


def _ffn_out_kernel(res_ref, a_ref, w_ref, o_ref):
    o_ref[...] = res_ref[...] + jnp.dot(a_ref[...], w_ref[...], preferred_element_type=F32)


def _ffn_out(res, act, w, layer, tm, tn):
    m, d = res.shape
    f = act.shape[1]
    return pl.pallas_call(
        _ffn_out_kernel,
        grid=(m // tm, d // tn),
        in_specs=[
            pl.BlockSpec((tm, tn), lambda i, j: (i, j)),
            pl.BlockSpec((tm, f), lambda i, j: (i, 0)),
            pl.BlockSpec((None, f, tn), lambda i, j: (layer, 0, j)),
        ],
        out_specs=pl.BlockSpec((tm, tn), lambda i, j: (i, j)),
        out_shape=jax.ShapeDtypeStruct((m, d), F32),
        compiler_params=_params(("parallel", "arbitrary")),
        name="ffn_out",
    )(res, act, w)


def _ffn_out_norm_kernel(res_ref, a_ref, w_ref, g_ref, o_ref, *, tn):
    j = pl.program_id(1)
    cols = pl.ds(pl.multiple_of(j * tn, tn), tn)
    o_ref[:, cols] = res_ref[...] + jnp.dot(a_ref[...], w_ref[...], preferred_element_type=F32)

    @pl.when(j == pl.num_programs(1) - 1)
    def _():
        o_ref[...] = _rmsnorm_rows(o_ref[...], g_ref[...])


def _ffn_out_norm(res, act, w, layer, g, tm, tn):
    m, d = res.shape
    f = act.shape[1]
    return pl.pallas_call(
        functools.partial(_ffn_out_norm_kernel, tn=tn),
        grid=(m // tm, d // tn),
        in_specs=[
            pl.BlockSpec((tm, tn), lambda i, j: (i, j)),
            pl.BlockSpec((tm, f), lambda i, j: (i, 0)),
            pl.BlockSpec((None, f, tn), lambda i, j: (layer, 0, j)),
            pl.BlockSpec((1, d), lambda i, j: (0, 0)),
        ],
        out_specs=pl.BlockSpec((tm, d), lambda i, j: (i, 0)),
        out_shape=jax.ShapeDtypeStruct((m, d), F32),
        compiler_params=_params(("parallel", "arbitrary")),
        name="ffn_out_norm",
    )(res, act, w, g.reshape(1, d))


def _lane_head(rows):
    return lax.shift_right_logical(lax.broadcasted_iota(jnp.int32, (rows, HEAD_GROUP_LANES), 1), 6)


def _stack_heads(q, lane_head):
    qf = q.astype(F32) * SCALE
    return jnp.concatenate(
        [jnp.where(lane_head == h, qf, 0.0) for h in range(A_HEADS_PER_GROUP)], axis=0
    ).astype(BF16)


def _alibi_slope(h):
    return 2.0 ** (-8.0 * (h + 1) / A_HEADS)


LANES = 128
BF16_ROWS = 16
A_UNROLL = 4
A_COPY_ROWS = 256
A_CASES = 3


def _lane_halves(x):
    return [x[:, lh * LANES:(lh + 1) * LANES] for lh in range(HEAD_GROUP_LANES // LANES)]


def _dilated_group(gi, win, dil, t, q_ref, k_ref, v_ref, qd_ref, kd_ref, vd_ref, stage_ref, bias_ref,
                   m_ref, num_ref, den_ref):
    n = t // dil
    half = (win // 2) // dil
    tq = min(2 * half, n)
    nk = min(4 * half, n)
    tiles_per = n // tq
    n_halves = HEAD_GROUP_LANES // LANES

    if dil > 1:
        cr = min(A_COPY_ROWS, n)
        chunks_per = n // cr
        for src, dst in ((q_ref, qd_ref), (k_ref, kd_ref), (v_ref, vd_ref)):
            def stage(c, carry, src=src):
                c0 = pl.multiple_of(c * A_COPY_ROWS, A_COPY_ROWS)
                rows = _lane_halves(src[pl.ds(c0, A_COPY_ROWS), :].astype(F32))
                for lh in range(n_halves):
                    stage_ref[lh, pl.ds(c0, A_COPY_ROWS), :] = rows[lh]
                return carry

            lax.fori_loop(0, t // A_COPY_ROWS, stage, 0)

            def gather(c, carry, dst=dst):
                r = c // chunks_per
                u0 = pl.multiple_of((c % chunks_per) * cr, cr)
                rows = jnp.concatenate(
                    [stage_ref[lh, pl.ds(r + u0 * dil, cr, stride=dil), :] for lh in range(n_halves)], axis=1)
                dst[pl.ds(pl.multiple_of(r * n + u0, cr), cr), :] = rows.astype(BF16)
                return carry

            lax.fori_loop(0, dil * chunks_per, gather, 0)
        q_src, k_src, v_src = qd_ref, kd_ref, vd_ref
    else:
        q_src, k_src, v_src = q_ref, k_ref, v_ref

    row_iota = lax.broadcasted_iota(jnp.int32, (tq, nk), 0)
    col_iota = lax.broadcasted_iota(jnp.int32, (tq, nk), 1)
    for case in range(A_CASES):
        rel = jnp.abs(col_iota - case * half - row_iota)
        relf = rel.astype(F32)
        for h in range(A_HEADS_PER_GROUP):
            coef = _alibi_slope(gi * A_HEADS_PER_GROUP + h) * dil
            bias_ref[case * A_HEADS_PER_GROUP + h, 0:tq, 0:nk] = jnp.where(rel <= half, -coef * relf, NEG)

    lane_head = _lane_head(tq)

    def body(tile, carry):
        r = tile // tiles_per
        i = tile % tiles_per
        q0 = pl.multiple_of(i * tq, tq)
        start = pl.multiple_of(jnp.clip(q0 - half, 0, n - nk), half) if n > nk else 0
        case = (q0 - start) // half
        base = pl.multiple_of(r * n, tq)
        qs = _stack_heads(q_src[pl.ds(base + q0, tq), :], lane_head)
        k0 = pl.multiple_of(base + start, BF16_ROWS)
        kw = k_src[pl.ds(k0, nk), :]
        vw = v_src[pl.ds(k0, nk), :]
        s = lax.dot_general(qs, kw, (((1,), (1,)), ((), ())), preferred_element_type=F32)
        mx = jnp.zeros((tq, HEAD_GROUP_LANES), F32)
        den = jnp.zeros((tq, HEAD_GROUP_LANES), F32)
        num = jnp.zeros((tq, HEAD_GROUP_LANES), F32)
        for h in range(A_HEADS_PER_GROUP):
            sh = s[h * tq:(h + 1) * tq] + bias_ref[case * A_HEADS_PER_GROUP + h, 0:tq, 0:nk]
            m = jnp.max(sh, axis=-1, keepdims=True)
            p = jnp.exp(sh - m)
            l = jnp.sum(p, axis=-1, keepdims=True)
            pv = jnp.dot(p.astype(BF16), vw, preferred_element_type=F32)
            sel = lane_head == h
            num = jnp.where(sel, pv, num)
            den = jnp.where(sel, l, den)
            mx = jnp.where(sel, m, mx)
        for lh, (mx_h, num_h, den_h) in enumerate(zip(_lane_halves(mx), _lane_halves(num), _lane_halves(den))):
            if gi == 0:
                idx = pl.ds(q0, tq)
                m_ref[lh, idx, :] = mx_h
                num_ref[lh, idx, :] = num_h
                den_ref[lh, idx, :] = den_h
            else:
                idx = pl.ds(q0 * dil + r, tq, stride=dil)
                m_old = m_ref[lh, idx, :]
                m_new = jnp.maximum(m_old, mx_h)
                s_old = jnp.exp(m_old - m_new)
                s_new = jnp.exp(mx_h - m_new)
                num_ref[lh, idx, :] = num_ref[lh, idx, :] * s_old + num_h * s_new
                den_ref[lh, idx, :] = den_ref[lh, idx, :] * s_old + den_h * s_new
                m_ref[lh, idx, :] = m_new
        return carry

    lax.fori_loop(0, dil * tiles_per, body, 0, unroll=A_UNROLL)


def _dilated_kernel(q_ref, k_ref, v_ref, o_ref, qd_ref, kd_ref, vd_ref, stage_ref, bias_ref, m_ref, num_ref,
                    den_ref, *, t):
    g = pl.program_id(1)
    for gi, (win, dil) in enumerate(A_PATTERNS):
        @pl.when(g == gi)
        def _(gi=gi, win=win, dil=dil):
            _dilated_group(gi, win, dil, t, q_ref, k_ref, v_ref, qd_ref, kd_ref, vd_ref, stage_ref, bias_ref,
                           m_ref, num_ref, den_ref)

    @pl.when(g == len(A_PATTERNS) - 1)
    def _():
        def emit(c, carry):
            idx = pl.ds(pl.multiple_of(c * A_COPY_ROWS, A_COPY_ROWS), A_COPY_ROWS)
            o_ref[idx, :] = jnp.concatenate(
                [num_ref[lh, idx, :] / den_ref[lh, idx, :] for lh in range(HEAD_GROUP_LANES // LANES)],
                axis=1).astype(o_ref.dtype)
            return carry

        lax.fori_loop(0, t // A_COPY_ROWS, emit, 0)


def _mixer_dilated(proj3):
    bsz, t, _ = proj3.shape
    halves = {(win // 2) // dil for win, dil in A_PATTERNS}
    assert len(halves) == 1, "bias table scratch is sized for one band half-width"
    half = halves.pop()

    def col_spec(off):
        return pl.BlockSpec((None, t, HEAD_GROUP_LANES), lambda b, g: (b, 0, off + g))

    seq_bf16 = pltpu.VMEM((t, HEAD_GROUP_LANES), BF16)
    seq_halves = pltpu.VMEM((HEAD_GROUP_LANES // LANES, t, LANES), F32)
    return pl.pallas_call(
        functools.partial(_dilated_kernel, t=t),
        grid=(bsz, len(A_PATTERNS)),
        in_specs=[col_spec(COL_QA), col_spec(COL_KA), col_spec(COL_VA)],
        out_specs=pl.BlockSpec((None, t, A_OUT), lambda b, g: (b, 0, 0)),
        out_shape=jax.ShapeDtypeStruct((bsz, t, A_OUT), BF16),
        scratch_shapes=[seq_bf16, seq_bf16, seq_bf16, seq_halves,
                        pltpu.VMEM((A_CASES * A_HEADS_PER_GROUP, 2 * half, 4 * half), F32),
                        seq_halves, seq_halves, seq_halves],
        compiler_params=_params(("parallel", "arbitrary")),
        name="dilated_attn",
    )(proj3, proj3, proj3)


B_TILE = 256
B_CHUNK = 256
B_PAD = 8
SUBLANES = 8
B_SCAN_UNROLL = 4


def _scan8(a, b, reverse):
    row = lax.broadcasted_iota(jnp.int32, a.shape, 0)
    for sh in (1, 2, 4):
        if reverse:
            a_s = pltpu.roll(a, SUBLANES - sh, 0)
            b_s = pltpu.roll(b, SUBLANES - sh, 0)
            ok = row < SUBLANES - sh
        else:
            a_s = pltpu.roll(a, sh, 0)
            b_s = pltpu.roll(b, sh, 0)
            ok = row >= sh
        a_s = jnp.where(ok, a_s, 1.0)
        b_s = jnp.where(ok, b_s, 0.0)
        b = a * b_s + b
        a = a * a_s
    return a, b


def _softplus(z):
    return jnp.maximum(z, 0.0) + jnp.log1p(jnp.exp(-jnp.abs(z)))


def _rglru_kernel(gb_ref, xb_ref, wg_ref, pk_ref, o_ref, xs_ref, af_ref, bf_ref, ab_ref, bb_ref, *, t):
    ct = B_TILE
    pk = pk_ref[...]
    conv_w = [pk[j:j + 1, :] for j in range(CONV_W)]
    conv_b = pk[4:5, :]
    bias = [pk[5 + j:6 + j, :] for j in range(4)]
    sp = [_softplus(-pk[9 + j:10 + j, :]) for j in range(2)]

    xs_ref[0:B_PAD, :] = jnp.zeros((B_PAD, ct), F32)
    xs_ref[B_PAD + t:2 * B_PAD + t, :] = jnp.zeros((B_PAD, ct), F32)

    def fill(c, carry):
        c0 = pl.multiple_of(c * B_CHUNK, B_CHUNK)
        xs_ref[pl.ds(c0 + B_PAD, B_CHUNK), :] = xb_ref[pl.ds(c0, B_CHUNK), :].astype(F32)
        return carry

    lax.fori_loop(0, t // B_CHUNK, fill, 0)

    def gates(c, carry):
        c0 = pl.multiple_of(c * B_CHUNK, B_CHUNK)
        slab = xs_ref[pl.ds(c0, B_CHUNK + 2 * B_PAD), :]
        left = CONV_W // 2
        xc = conv_b + conv_w[0] * slab[B_PAD - left:B_PAD - left + B_CHUNK]
        for j in range(1, CONV_W):
            xc = xc + conv_w[j] * slab[B_PAD - left + j:B_PAD - left + j + B_CHUNK]
        g = jnp.dot(xc.astype(BF16), wg_ref[...], preferred_element_type=F32)
        for d, (a_ref, b_ref) in enumerate(((af_ref, bf_ref), (ab_ref, bb_ref))):
            r = jax.nn.sigmoid(g[:, (2 * d) * ct:(2 * d + 1) * ct] + bias[2 * d])
            i = jax.nn.sigmoid(g[:, (2 * d + 1) * ct:(2 * d + 2) * ct] + bias[2 * d + 1])
            log_a = -RG_C * r * sp[d]
            a = jnp.exp(log_a)
            one_minus_a2 = -jnp.tanh(log_a) * (a * a + 1.0)
            a_ref[pl.ds(c0, B_CHUNK), :] = a
            b_ref[pl.ds(c0, B_CHUNK), :] = jnp.sqrt(one_minus_a2) * (i * xc)
        return carry

    lax.fori_loop(0, t // B_CHUNK, gates, 0)

    def scan(k, carry):
        cf, cb = carry
        rf = pl.multiple_of(k * SUBLANES, SUBLANES)
        rb = pl.multiple_of(t - SUBLANES - k * SUBLANES, SUBLANES)
        a, b = _scan8(af_ref[pl.ds(rf, SUBLANES), :], bf_ref[pl.ds(rf, SUBLANES), :], False)
        hf = a * cf + b
        bf_ref[pl.ds(rf, SUBLANES), :] = hf
        a, b = _scan8(ab_ref[pl.ds(rb, SUBLANES), :], bb_ref[pl.ds(rb, SUBLANES), :], True)
        hb = a * cb + b
        bb_ref[pl.ds(rb, SUBLANES), :] = hb
        return hf[SUBLANES - 1:SUBLANES, :], hb[0:1, :]

    zero = jnp.zeros((1, ct), F32)
    lax.fori_loop(0, t // SUBLANES, scan, (zero, zero), unroll=B_SCAN_UNROLL)

    def emit(c, carry):
        c0 = pl.multiple_of(c * B_CHUNK, B_CHUNK)
        h = bf_ref[pl.ds(c0, B_CHUNK), :] + bb_ref[pl.ds(c0, B_CHUNK), :]
        gate = gb_ref[pl.ds(c0, B_CHUNK), :].astype(F32)
        o_ref[pl.ds(c0, B_CHUNK), :] = (jax.nn.gelu(gate) * h).astype(o_ref.dtype)
        return carry

    lax.fori_loop(0, t // B_CHUNK, emit, 0)


def _rglru_weights(conv_w, conv_b, wa, ba, wx, bx, lam):
    eye = jnp.eye(B_BLOCKS, dtype=F32)

    def dense(w):
        return jnp.einsum("ncd,nm->ncmd", w, eye).reshape(B_WIDTH, B_WIDTH)

    mats = [dense(wa[0]), dense(wx[0]), dense(wa[1]), dense(wx[1])]
    nt = B_WIDTH // B_TILE
    wg = jnp.stack([
        jnp.concatenate([m[c * B_TILE:(c + 1) * B_TILE, c * B_TILE:(c + 1) * B_TILE] for m in mats], axis=1)
        for c in range(nt)
    ]).astype(BF16)
    rows = [conv_w[j] for j in range(CONV_W)] + [conv_b, ba[0], bx[0], ba[1], bx[1], lam[0], lam[1]]
    rows += [jnp.zeros_like(conv_b)] * (16 - len(rows))
    pk = jnp.stack(rows).reshape(16, nt, B_TILE).transpose(1, 0, 2)
    return wg, pk


def _mixer_rglru(proj3, wg, pk):
    bsz, t, _ = proj3.shape
    nt = B_WIDTH // B_TILE
    return pl.pallas_call(
        functools.partial(_rglru_kernel, t=t),
        grid=(bsz, nt),
        in_specs=[
            pl.BlockSpec((None, t, B_TILE), lambda b, c: (b, 0, COL_GB + c)),
            pl.BlockSpec((None, t, B_TILE), lambda b, c: (b, 0, COL_XB + c)),
            pl.BlockSpec((None, B_TILE, 4 * B_TILE), lambda b, c: (c, 0, 0)),
            pl.BlockSpec((None, 16, B_TILE), lambda b, c: (c, 0, 0)),
        ],
        out_specs=pl.BlockSpec((None, t, B_TILE), lambda b, c: (b, 0, c)),
        out_shape=jax.ShapeDtypeStruct((bsz, t, B_WIDTH), BF16),
        scratch_shapes=[pltpu.VMEM((t + 2 * B_PAD, B_TILE), F32)] + [pltpu.VMEM((t, B_TILE), F32)] * 4,
        compiler_params=_params(("parallel", "parallel")),
        name="rglru",
    )(proj3, proj3, wg, pk)


NA_KEYS = NA_KH * GRID_W
NA_UNROLL = 8


NA_PAIR = LANES // GRID_W


def _na_bias_tables(rpb):
    c = np.arange(GRID_W)
    cs = np.clip(c - NA_KW // 2, 0, GRID_W - NA_KW)
    kc = np.arange(GRID_W)
    inwin = (kc[None, :] >= cs[:, None]) & (kc[None, :] < cs[:, None] + NA_KW)
    off = np.clip(kc[None, :] - c[:, None] + NA_KW - 1, 0, 2 * NA_KW - 2)
    onehot = (off[None] == np.arange(2 * NA_KW - 1)[:, None, None]).astype(np.float32)
    tb = jnp.einsum("hdo,ock->hdck", rpb.astype(F32), jnp.asarray(onehot), precision=lax.Precision.HIGHEST)
    tb = jnp.where(inwin[None, None], tb, NEG)
    n_d = tb.shape[1] - (NA_PAIR - 1)
    pair = jnp.concatenate([tb[:, e:e + n_d] for e in range(NA_PAIR)], axis=-1)
    pair = pair.reshape(C_HEADS // 4, 4, n_d, GRID_W, LANES).transpose(0, 2, 1, 3, 4)
    return pair.reshape(C_HEADS // 4, n_d, 4 * GRID_W, LANES)


def _na_kernel(q_ref, k_ref, v_ref, tbl_ref, o_ref, *, rows):
    lane_head = _lane_head(GRID_W)

    def body(r, carry):
        rs = jnp.clip(r - NA_KH // 2, 0, rows - NA_KH)
        dr0 = rs - r + NA_KH - 1
        q0 = pl.multiple_of(r * GRID_W, GRID_W)
        k0 = pl.multiple_of(rs * GRID_W, GRID_W)
        qs = _stack_heads(q_ref[pl.ds(q0, GRID_W), :], lane_head)
        kw = k_ref[pl.ds(k0, NA_KEYS), :]
        vw = v_ref[pl.ds(k0, NA_KEYS), :]
        s = lax.dot_general(qs, kw, (((1,), (1,)), ((), ())), preferred_element_type=F32)
        s = s + jnp.concatenate([tbl_ref[dr0 + NA_PAIR * k] for k in range(NA_KH // NA_PAIR)], axis=1)
        m = jnp.max(s, axis=-1, keepdims=True)
        p = jnp.exp(s - m)
        l = jnp.sum(p, axis=-1, keepdims=True)
        pv = jnp.dot(p.astype(BF16), vw, preferred_element_type=F32) / l
        o = pv[0:GRID_W]
        for h in range(1, 4):
            o = jnp.where(lane_head == h, pv[h * GRID_W:(h + 1) * GRID_W], o)
        o_ref[pl.ds(q0, GRID_W), :] = o.astype(o_ref.dtype)
        return carry

    lax.fori_loop(0, rows, body, 0, unroll=NA_UNROLL)


def _mixer_na(proj3, tbl):
    bsz, t, _ = proj3.shape
    rows = t // GRID_W
    nhg = C_HEADS // 4

    def col_spec(off):
        return pl.BlockSpec((None, t, HEAD_GROUP_LANES), lambda hg, b: (b, 0, off + hg))

    return pl.pallas_call(
        functools.partial(_na_kernel, rows=rows),
        grid=(nhg, bsz),
        in_specs=[col_spec(COL_QC), col_spec(COL_KC), col_spec(COL_VC),
                  pl.BlockSpec((None,) + tbl.shape[1:], lambda hg, b: (hg, 0, 0, 0))],
        out_specs=pl.BlockSpec((None, t, HEAD_GROUP_LANES), lambda hg, b: (b, 0, hg)),
        out_shape=jax.ShapeDtypeStruct((bsz, t, C_WIDTH), BF16),
        compiler_params=_params(("parallel", "parallel")),
        name="na_attn",
    )(proj3, proj3, proj3, tbl)


def _tile(m, want):
    return want if m % want == 0 else m


TILE_PROJ = (1024, D_IN // 2)
TILE_MIX_OUT_ROWS = 512
TILE_SWIGLU = (2048, 512)
TILE_FFN_OUT = (1024, 512)
TILE_FFN_OUT_NORM = (512, 1024)


def _encode(x, w):
    bsz, t, d = x.shape
    m = bsz * t
    xf = x.reshape(m, d)
    for l in range(DEPTH):
        proj = _norm_proj(xf, w["norm1_g"][l], w["w_in"], l, _tile(m, TILE_PROJ[0]), TILE_PROJ[1])
        proj3 = proj.reshape(bsz, t, D_IN)
        o_a = _mixer_dilated(proj3)
        o_b = _mixer_rglru(proj3, w["rg_wg"][l], w["rg_pk"][l])
        o_c = _mixer_na(proj3, w["na_tbl"][l])
        xf, xn = _mix_out(xf, o_a.reshape(m, A_OUT), o_b.reshape(m, B_WIDTH), o_c.reshape(m, C_WIDTH),
                          w["w_out"], l, w["norm2_g"][l], _tile(m, TILE_MIX_OUT_ROWS))
        act = _swiglu(xn, w["w_ffn_in"], l, _tile(m, TILE_SWIGLU[0]), TILE_SWIGLU[1])
        if l < DEPTH - 1:
            xf = _ffn_out(xf, act, w["w_ffn_out"], l, _tile(m, TILE_FFN_OUT[0]), TILE_FFN_OUT[1])
        else:
            xf = _ffn_out_norm(xf, act, w["w_ffn_out"], l, w["final_g"], _tile(m, TILE_FFN_OUT_NORM[0]),
                               TILE_FFN_OUT_NORM[1])
    return xf.reshape(bsz, t, d)


def kernel(x_prompt, x_sample, norm1_g, w_in, conv_w, conv_b, rg_wa, rg_ba, rg_wx, rg_bx, rg_lam, na_rpb,
           w_out, norm2_g, w_ffn_in, w_ffn_out, final_g):
    rg = [_rglru_weights(conv_w[l], conv_b[l], rg_wa[l], rg_ba[l], rg_wx[l], rg_bx[l], rg_lam[l])
          for l in range(DEPTH)]
    w = {
        "norm1_g": norm1_g, "norm2_g": norm2_g, "final_g": final_g,
        "w_in": w_in.astype(BF16), "w_out": w_out.astype(BF16),
        "w_ffn_in": w_ffn_in.astype(BF16), "w_ffn_out": w_ffn_out.astype(BF16),
        "rg_wg": [r[0] for r in rg], "rg_pk": [r[1] for r in rg],
        "na_tbl": [_na_bias_tables(na_rpb[l]) for l in range(DEPTH)],
    }
    return _encode(x_prompt, w), _encode(x_sample, w)
```

```python
import functools

import numpy as np
import jax
import jax.numpy as jnp
from jax import lax
from jax.experimental import pallas as pl
from jax.experimental.pallas import tpu as pltpu

D_MODEL = 2048
DEPTH = 2
HEAD_DIM = 64
A_PATTERNS = ((128, 1), (512, 4), (2048, 16))
A_HEADS_PER_GROUP = 4
A_HEADS = A_HEADS_PER_GROUP * len(A_PATTERNS)
A_WIDTH = A_HEADS * HEAD_DIM
A_OUT = A_HEADS_PER_GROUP * HEAD_DIM
B_WIDTH = 512
B_BLOCKS = 8
CONV_W = 4
RG_C = 8.0
C_HEADS = 12
C_WIDTH = C_HEADS * HEAD_DIM
GRID_W = 64
NA_KH = 8
NA_KW = 16
D_FF = 5632
D_IN = 3 * A_WIDTH + 2 * B_WIDTH + 3 * C_WIDTH
D_MIX_OUT = A_OUT + B_WIDTH + C_WIDTH
NORM_EPS = 1e-6

F32 = jnp.float32
BF16 = jnp.bfloat16

HEAD_GROUP_LANES = 4 * HEAD_DIM
SCALE = HEAD_DIM ** -0.5
NEG = -1e30

COL_QA, COL_KA, COL_VA = 0, 3, 6
COL_GB, COL_XB = 9, 11
COL_QC, COL_KC, COL_VC = 13, 16, 19

MIB = 1024 * 1024
VMEM_LIMIT = 56 * MIB
VMEM_INTERNAL = 5 * MIB


def _params(sem, vmem_bytes=VMEM_LIMIT):
    return pltpu.CompilerParams(dimension_semantics=sem, vmem_limit_bytes=vmem_bytes)


def _rmsnorm_rows(x, g):
    ms = jnp.mean(x * x, axis=-1, keepdims=True)
    return x * lax.rsqrt(ms + NORM_EPS) * g


DENSE_ROWS = 256


def _row_chunks(n_rows):
    return [slice(r0, min(r0 + DENSE_ROWS, n_rows)) for r0 in range(0, n_rows, DENSE_ROWS)]


def _norm_proj_kernel(x_ref, g_ref, w_ref, o_ref, xn_ref):
    @pl.when(pl.program_id(1) == 0)
    def _():
        for rows in _row_chunks(x_ref.shape[0]):
            xn = _rmsnorm_rows(x_ref[rows, :], g_ref[...]).astype(BF16)
            xn_ref[rows, :] = xn
            o_ref[rows, :] = jnp.dot(xn, w_ref[...], preferred_element_type=F32).astype(o_ref.dtype)

    @pl.when(pl.program_id(1) != 0)
    def _():
        for rows in _row_chunks(x_ref.shape[0]):
            o_ref[rows, :] = jnp.dot(xn_ref[rows, :], w_ref[...], preferred_element_type=F32).astype(o_ref.dtype)


def _norm_proj(x, g, w, layer, tm, tn):
    m, d = x.shape
    n = w.shape[2]
    blocks = 2 * (tm * d * 4 + d * tn * 2 + tm * tn * 2) + tm * d * 2
    return pl.pallas_call(
        _norm_proj_kernel,
        grid=(m // tm, n // tn),
        in_specs=[
            pl.BlockSpec((tm, d), lambda i, j: (i, 0)),
            pl.BlockSpec((1, d), lambda i, j: (0, 0)),
            pl.BlockSpec((None, d, tn), lambda i, j: (layer, 0, j)),
        ],
        out_specs=pl.BlockSpec((tm, tn), lambda i, j: (i, j)),
        out_shape=jax.ShapeDtypeStruct((m, n), BF16),
        scratch_shapes=[pltpu.VMEM((tm, d), BF16)],
        compiler_params=_params(("parallel", "arbitrary"), blocks + VMEM_INTERNAL),
        name="norm_proj",
    )(x, g.reshape(1, d), w)


def _swiglu_kernel(xn_ref, wg_ref, wu_ref, o_ref):
    for rows in _row_chunks(xn_ref.shape[0]):
        xn = xn_ref[rows, :]
        gate = jnp.dot(xn, wg_ref[...], preferred_element_type=F32)
        up = jnp.dot(xn, wu_ref[...], preferred_element_type=F32)
        o_ref[rows, :] = (gate * jax.nn.sigmoid(gate) * up).astype(o_ref.dtype)


def _swiglu(xn, w, layer, tm, tn):
    m, d = xn.shape
    f = w.shape[2] // 2
    nf = f // tn
    return pl.pallas_call(
        _swiglu_kernel,
        grid=(m // tm, nf),
        in_specs=[
            pl.BlockSpec((tm, d), lambda i, j: (i, 0)),
            pl.BlockSpec((None, d, tn), lambda i, j: (layer, 0, j)),
            pl.BlockSpec((None, d, tn), lambda i, j: (layer, 0, j + nf)),
        ],
        out_specs=pl.BlockSpec((tm, tn), lambda i, j: (i, j)),
        out_shape=jax.ShapeDtypeStruct((m, f), BF16),
        compiler_params=_params(("parallel", "arbitrary")),
        name="swiglu",
    )(xn, w, w)


def _mix_out_kernel(res_ref, a_ref, b_ref, c_ref, w_ref, g_ref, o_ref, xn_ref):
    for rows in _row_chunks(res_ref.shape[0]):
        acc = res_ref[rows, :]
        acc = acc + jnp.dot(a_ref[rows, :], w_ref[0:A_OUT, :], preferred_element_type=F32)
        acc = acc + jnp.dot(b_ref[rows, :], w_ref[A_OUT:A_OUT + B_WIDTH, :], preferred_element_type=F32)
        acc = acc + jnp.dot(c_ref[rows, :], w_ref[A_OUT + B_WIDTH:, :], preferred_element_type=F32)
        o_ref[rows, :] = acc
        xn_ref[rows, :] = _rmsnorm_rows(acc, g_ref[...]).astype(BF16)


def _mix_out(res, o_a, o_b, o_c, w, layer, g, tm):
    m, d = res.shape
    row = lambda width: pl.BlockSpec((tm, width), lambda i: (i, 0))
    return pl.pallas_call(
        _mix_out_kernel,
        grid=(m // tm,),
        in_specs=[row(d), row(A_OUT), row(B_WIDTH), row(C_WIDTH),
                  pl.BlockSpec((None, D_MIX_OUT, d), lambda i: (layer, 0, 0)),
                  pl.BlockSpec((1, d), lambda i: (0, 0))],
        out_specs=(row(d), row(d)),
        out_shape=(jax.ShapeDtypeStruct((m, d), F32), jax.ShapeDtypeStruct((m, d), BF16)),
        compiler_params=_params(("parallel",)),
        name="mix_out",
    )(res, o_a, o_b, o_c, w, g.reshape(1, d))


def _ffn_out_kernel(res_ref, a_ref, w_ref, o_ref):
    o_ref[...] = res_ref[...] + jnp.dot(a_ref[...], w_ref[...], preferred_element_type=F32)


def _ffn_out(res, act, w, layer, tm, tn):
    m, d = res.shape
    f = act.shape[1]
    return pl.pallas_call(
        _ffn_out_kernel,
        grid=(m // tm, d // tn),
        in_specs=[
            pl.BlockSpec((tm, tn), lambda i, j: (i, j)),
            pl.BlockSpec((tm, f), lambda i, j: (i, 0)),
            pl.BlockSpec((None, f, tn), lambda i, j: (layer, 0, j)),
        ],
        out_specs=pl.BlockSpec((tm, tn), lambda i, j: (i, j)),
        out_shape=jax.ShapeDtypeStruct((m, d), F32),
        compiler_params=_params(("parallel", "arbitrary")),
        name="ffn_out",
    )(res, act, w)


def _ffn_out_norm_kernel(res_ref, a_ref, w_ref, g_ref, o_ref, *, tn):
    j = pl.program_id(1)
    cols = pl.ds(pl.multiple_of(j * tn, tn), tn)
    o_ref[:, cols] = res_ref[...] + jnp.dot(a_ref[...], w_ref[...], preferred_element_type=F32)

    @pl.when(j == pl.num_programs(1) - 1)
    def _():
        o_ref[...] = _rmsnorm_rows(o_ref[...], g_ref[...])


def _ffn_out_norm(res, act, w, layer, g, tm, tn):
    m, d = res.shape
    f = act.shape[1]
    return pl.pallas_call(
        functools.partial(_ffn_out_norm_kernel, tn=tn),
        grid=(m // tm, d // tn),
        in_specs=[
            pl.BlockSpec((tm, tn), lambda i, j: (i, j)),
            pl.BlockSpec((tm, f), lambda i, j: (i, 0)),
            pl.BlockSpec((None, f, tn), lambda i, j: (layer, 0, j)),
            pl.BlockSpec((1, d), lambda i, j: (0, 0)),
        ],
        out_specs=pl.BlockSpec((tm, d), lambda i, j: (i, 0)),
        out_shape=jax.ShapeDtypeStruct((m, d), F32),
        compiler_params=_params(("parallel", "arbitrary")),
        name="ffn_out_norm",
    )(res, act, w, g.reshape(1, d))


def _lane_head(rows):
    return lax.shift_right_logical(lax.broadcasted_iota(jnp.int32, (rows, HEAD_GROUP_LANES), 1), 6)


def _stack_heads(q, lane_head):
    qf = q.astype(F32) * SCALE
    return jnp.concatenate(
        [jnp.where(lane_head == h, qf, 0.0) for h in range(A_HEADS_PER_GROUP)], axis=0
    ).astype(BF16)


def _alibi_slope(h):
    return 2.0 ** (-8.0 * (h + 1) / A_HEADS)


LANES = 128
BF16_ROWS = 16
A_UNROLL = 8
A_COPY_ROWS = 256
A_CASES = 3


def _lane_halves(x):
    return [x[:, lh * LANES:(lh + 1) * LANES] for lh in range(HEAD_GROUP_LANES // LANES)]


def _dilated_group(gi, win, dil, t, q_ref, k_ref, v_ref, qd_ref, kd_ref, vd_ref, stage_ref, bias_ref,
                   m_ref, num_ref, den_ref):
    n = t // dil
    half = (win // 2) // dil
    tq = min(2 * half, n)
    nk = min(4 * half, n)
    tiles_per = n // tq
    n_halves = HEAD_GROUP_LANES // LANES

    if dil > 1:
        cr = min(A_COPY_ROWS, n)
        chunks_per = n // cr
        for src, dst in ((q_ref, qd_ref), (k_ref, kd_ref), (v_ref, vd_ref)):
            def stage(c, carry, src=src):
                c0 = pl.multiple_of(c * A_COPY_ROWS, A_COPY_ROWS)
                rows = _lane_halves(src[pl.ds(c0, A_COPY_ROWS), :].astype(F32))
                for lh in range(n_halves):
                    stage_ref[lh, pl.ds(c0, A_COPY_ROWS), :] = rows[lh]
                return carry

            lax.fori_loop(0, t // A_COPY_ROWS, stage, 0)

            def gather(c, carry, dst=dst):
                r = c // chunks_per
                u0 = pl.multiple_of((c % chunks_per) * cr, cr)
                rows = jnp.concatenate(
                    [stage_ref[lh, pl.ds(r + u0 * dil, cr, stride=dil), :] for lh in range(n_halves)], axis=1)
                dst[pl.ds(pl.multiple_of(r * n + u0, cr), cr), :] = rows.astype(BF16)
                return carry

            lax.fori_loop(0, dil * chunks_per, gather, 0)
        q_src, k_src, v_src = qd_ref, kd_ref, vd_ref
    else:
        q_src, k_src, v_src = q_ref, k_ref, v_ref

    row_iota = lax.broadcasted_iota(jnp.int32, (tq, nk), 0)
    col_iota = lax.broadcasted_iota(jnp.int32, (tq, nk), 1)
    for case in range(A_CASES):
        rel = jnp.abs(col_iota - case * half - row_iota)
        relf = rel.astype(F32)
        for h in range(A_HEADS_PER_GROUP):
            coef = _alibi_slope(gi * A_HEADS_PER_GROUP + h) * dil
            bias_ref[case * A_HEADS_PER_GROUP + h, 0:tq, 0:nk] = jnp.where(rel <= half, -coef * relf, NEG)

    lane_head = _lane_head(tq)

    def body(tile, carry):
        r = tile // tiles_per
        i = tile % tiles_per
        q0 = pl.multiple_of(i * tq, tq)
        start = pl.multiple_of(jnp.clip(q0 - half, 0, n - nk), half) if n > nk else 0
        case = (q0 - start) // half
        base = pl.multiple_of(r * n, tq)
        qs = _stack_heads(q_src[pl.ds(base + q0, tq), :], lane_head)
        k0 = pl.multiple_of(base + start, BF16_ROWS)
        kw = k_src[pl.ds(k0, nk), :]
        vw = v_src[pl.ds(k0, nk), :]
        s = lax.dot_general(qs, kw, (((1,), (1,)), ((), ())), preferred_element_type=F32)
        heads_per_half = LANES // HEAD_DIM
        first_head = lax.broadcasted_iota(jnp.int32, (tq, LANES), 1) < HEAD_DIM
        for lh in range(n_halves):
            parts = []
            for h in range(lh * heads_per_half, (lh + 1) * heads_per_half):
                sh = s[h * tq:(h + 1) * tq] + bias_ref[case * A_HEADS_PER_GROUP + h, 0:tq, 0:nk]
                m = jnp.max(sh, axis=-1, keepdims=True)
                p = jnp.exp(sh - m)
                l = jnp.sum(p, axis=-1, keepdims=True)
                pv = jnp.dot(p.astype(BF16), vw, preferred_element_type=F32)
                parts.append((jnp.broadcast_to(m, (tq, LANES)), jnp.broadcast_to(l, (tq, LANES)),
                              pv[:, lh * LANES:(lh + 1) * LANES]))
            (m0, l0, pv0), (m1, l1, pv1) = parts
            mx_h = jnp.where(first_head, m0, m1)
            den_h = jnp.where(first_head, l0, l1)
            num_h = jnp.where(first_head, pv0, pv1)
            if gi == 0:
                idx = pl.ds(q0, tq)
                m_ref[lh, idx, :] = mx_h
                num_ref[lh, idx, :] = num_h
                den_ref[lh, idx, :] = den_h
            else:
                idx = pl.ds(q0 * dil + r, tq, stride=dil)
                m_old = m_ref[lh, idx, :]
                m_new = jnp.maximum(m_old, mx_h)
                s_old = jnp.exp(m_old - m_new)
                s_new = jnp.exp(mx_h - m_new)
                num_ref[lh, idx, :] = num_ref[lh, idx, :] * s_old + num_h * s_new
                den_ref[lh, idx, :] = den_ref[lh, idx, :] * s_old + den_h * s_new
                m_ref[lh, idx, :] = m_new
        return carry

    lax.fori_loop(0, dil * tiles_per, body, 0, unroll=A_UNROLL)


def _dilated_kernel(q_ref, k_ref, v_ref, o_ref, qd_ref, kd_ref, vd_ref, stage_ref, bias_ref, m_ref, num_ref,
                    den_ref, *, t):
    g = pl.program_id(1)
    for gi, (win, dil) in enumerate(A_PATTERNS):
        @pl.when(g == gi)
        def _(gi=gi, win=win, dil=dil):
            _dilated_group(gi, win, dil, t, q_ref, k_ref, v_ref, qd_ref, kd_ref, vd_ref, stage_ref, bias_ref,
                           m_ref, num_ref, den_ref)

    @pl.when(g == len(A_PATTERNS) - 1)
    def _():
        def emit(c, carry):
            idx = pl.ds(pl.multiple_of(c * A_COPY_ROWS, A_COPY_ROWS), A_COPY_ROWS)
            o_ref[idx, :] = jnp.concatenate(
                [num_ref[lh, idx, :] / den_ref[lh, idx, :] for lh in range(HEAD_GROUP_LANES // LANES)],
                axis=1).astype(o_ref.dtype)
            return carry

        lax.fori_loop(0, t // A_COPY_ROWS, emit, 0)


def _mixer_dilated(proj3):
    bsz, t, _ = proj3.shape
    halves = {(win // 2) // dil for win, dil in A_PATTERNS}
    assert len(halves) == 1, "bias table scratch is sized for one band half-width"
    half = halves.pop()

    def col_spec(off):
        return pl.BlockSpec((None, t, HEAD_GROUP_LANES), lambda b, g: (b, 0, off + g))

    seq_bf16 = pltpu.VMEM((t, HEAD_GROUP_LANES), BF16)
    seq_halves = pltpu.VMEM((HEAD_GROUP_LANES // LANES, t, LANES), F32)
    return pl.pallas_call(
        functools.partial(_dilated_kernel, t=t),
        grid=(bsz, len(A_PATTERNS)),
        in_specs=[col_spec(COL_QA), col_spec(COL_KA), col_spec(COL_VA)],
        out_specs=pl.BlockSpec((None, t, A_OUT), lambda b, g: (b, 0, 0)),
        out_shape=jax.ShapeDtypeStruct((bsz, t, A_OUT), BF16),
        scratch_shapes=[seq_bf16, seq_bf16, seq_bf16, seq_halves,
                        pltpu.VMEM((A_CASES * A_HEADS_PER_GROUP, 2 * half, 4 * half), F32),
                        seq_halves, seq_halves, seq_halves],
        compiler_params=_params(("parallel", "arbitrary")),
        name="dilated_attn",
    )(proj3, proj3, proj3)


B_TILE = 256
B_CHUNK = 256
B_PAD = 8
SUBLANES = 8
B_SCAN_UNROLL = 4


def _scan8(a, b, reverse):
    row = lax.broadcasted_iota(jnp.int32, a.shape, 0)
    for sh in (1, 2, 4):
        if reverse:
            a_s = pltpu.roll(a, SUBLANES - sh, 0)
            b_s = pltpu.roll(b, SUBLANES - sh, 0)
            ok = row < SUBLANES - sh
        else:
            a_s = pltpu.roll(a, sh, 0)
            b_s = pltpu.roll(b, sh, 0)
            ok = row >= sh
        a_s = jnp.where(ok, a_s, 1.0)
        b_s = jnp.where(ok, b_s, 0.0)
        b = a * b_s + b
        a = a * a_s
    return a, b


def _softplus(z):
    return jnp.maximum(z, 0.0) + jnp.log1p(jnp.exp(-jnp.abs(z)))


def _rglru_kernel(gb_ref, xb_ref, wg_ref, pk_ref, o_ref, xs_ref, af_ref, bf_ref, ab_ref, bb_ref, *, t):
    ct = B_TILE
    pk = pk_ref[...]
    conv_w = [pk[j:j + 1, :] for j in range(CONV_W)]
    conv_b = pk[4:5, :]
    bias = [pk[5 + j:6 + j, :] for j in range(4)]
    sp = [_softplus(-pk[9 + j:10 + j, :]) for j in range(2)]

    xs_ref[0:B_PAD, :] = jnp.zeros((B_PAD, ct), F32)
    xs_ref[B_PAD + t:2 * B_PAD + t, :] = jnp.zeros((B_PAD, ct), F32)

    def fill(c, carry):
        c0 = pl.multiple_of(c * B_CHUNK, B_CHUNK)
        xs_ref[pl.ds(c0 + B_PAD, B_CHUNK), :] = xb_ref[pl.ds(c0, B_CHUNK), :].astype(F32)
        return carry

    lax.fori_loop(0, t // B_CHUNK, fill, 0)

    def gates(c, carry):
        c0 = pl.multiple_of(c * B_CHUNK, B_CHUNK)
        slab = xs_ref[pl.ds(c0, B_CHUNK + 2 * B_PAD), :]
        left = CONV_W // 2
        xc = conv_b + conv_w[0] * slab[B_PAD - left:B_PAD - left + B_CHUNK]
        for j in range(1, CONV_W):
            xc = xc + conv_w[j] * slab[B_PAD - left + j:B_PAD - left + j + B_CHUNK]
        g = jnp.dot(xc.astype(BF16), wg_ref[...], preferred_element_type=F32)
        for d, (a_ref, b_ref) in enumerate(((af_ref, bf_ref), (ab_ref, bb_ref))):
            r = jax.nn.sigmoid(g[:, (2 * d) * ct:(2 * d + 1) * ct] + bias[2 * d])
            i = jax.nn.sigmoid(g[:, (2 * d + 1) * ct:(2 * d + 2) * ct] + bias[2 * d + 1])
            log_a = -RG_C * r * sp[d]
            a = jnp.exp(log_a)
            one_minus_a2 = -jnp.tanh(log_a) * (a * a + 1.0)
            a_ref[pl.ds(c0, B_CHUNK), :] = a
            b_ref[pl.ds(c0, B_CHUNK), :] = jnp.sqrt(one_minus_a2) * (i * xc)
        return carry

    lax.fori_loop(0, t // B_CHUNK, gates, 0)

    def scan(k, carry):
        cf, cb = carry
        rf = pl.multiple_of(k * SUBLANES, SUBLANES)
        rb = pl.multiple_of(t - SUBLANES - k * SUBLANES, SUBLANES)
        a, b = _scan8(af_ref[pl.ds(rf, SUBLANES), :], bf_ref[pl.ds(rf, SUBLANES), :], False)
        hf = a * cf + b
        bf_ref[pl.ds(rf, SUBLANES), :] = hf
        a, b = _scan8(ab_ref[pl.ds(rb, SUBLANES), :], bb_ref[pl.ds(rb, SUBLANES), :], True)
        hb = a * cb + b
        bb_ref[pl.ds(rb, SUBLANES), :] = hb
        return hf[SUBLANES - 1:SUBLANES, :], hb[0:1, :]

    zero = jnp.zeros((1, ct), F32)
    lax.fori_loop(0, t // SUBLANES, scan, (zero, zero), unroll=B_SCAN_UNROLL)

    def emit(c, carry):
        c0 = pl.multiple_of(c * B_CHUNK, B_CHUNK)
        h = bf_ref[pl.ds(c0, B_CHUNK), :] + bb_ref[pl.ds(c0, B_CHUNK), :]
        gate = gb_ref[pl.ds(c0, B_CHUNK), :].astype(F32)
        o_ref[pl.ds(c0, B_CHUNK), :] = (jax.nn.gelu(gate) * h).astype(o_ref.dtype)
        return carry

    lax.fori_loop(0, t // B_CHUNK, emit, 0)


def _rglru_weights(conv_w, conv_b, wa, ba, wx, bx, lam):
    eye = jnp.eye(B_BLOCKS, dtype=F32)

    def dense(w):
        return jnp.einsum("ncd,nm->ncmd", w, eye).reshape(B_WIDTH, B_WIDTH)

    mats = [dense(wa[0]), dense(wx[0]), dense(wa[1]), dense(wx[1])]
    nt = B_WIDTH // B_TILE
    wg = jnp.stack([
        jnp.concatenate([m[c * B_TILE:(c + 1) * B_TILE, c * B_TILE:(c + 1) * B_TILE] for m in mats], axis=1)
        for c in range(nt)
    ]).astype(BF16)
    rows = [conv_w[j] for j in range(CONV_W)] + [conv_b, ba[0], bx[0], ba[1], bx[1], lam[0], lam[1]]
    rows += [jnp.zeros_like(conv_b)] * (16 - len(rows))
    pk = jnp.stack(rows).reshape(16, nt, B_TILE).transpose(1, 0, 2)
    return wg, pk


def _mixer_rglru(proj3, wg, pk):
    bsz, t, _ = proj3.shape
    nt = B_WIDTH // B_TILE
    return pl.pallas_call(
        functools.partial(_rglru_kernel, t=t),
        grid=(bsz, nt),
        in_specs=[
            pl.BlockSpec((None, t, B_TILE), lambda b, c: (b, 0, COL_GB + c)),
            pl.BlockSpec((None, t, B_TILE), lambda b, c: (b, 0, COL_XB + c)),
            pl.BlockSpec((None, B_TILE, 4 * B_TILE), lambda b, c: (c, 0, 0)),
            pl.BlockSpec((None, 16, B_TILE), lambda b, c: (c, 0, 0)),
        ],
        out_specs=pl.BlockSpec((None, t, B_TILE), lambda b, c: (b, 0, c)),
        out_shape=jax.ShapeDtypeStruct((bsz, t, B_WIDTH), BF16),
        scratch_shapes=[pltpu.VMEM((t + 2 * B_PAD, B_TILE), F32)] + [pltpu.VMEM((t, B_TILE), F32)] * 4,
        compiler_params=_params(("parallel", "parallel")),
        name="rglru",
    )(proj3, proj3, wg, pk)


NA_KEYS = NA_KH * GRID_W
NA_UNROLL = 8


NA_PAIR = LANES // GRID_W


def _na_bias_tables(rpb):
    c = np.arange(GRID_W)
    cs = np.clip(c - NA_KW // 2, 0, GRID_W - NA_KW)
    kc = np.arange(GRID_W)
    inwin = (kc[None, :] >= cs[:, None]) & (kc[None, :] < cs[:, None] + NA_KW)
    off = np.clip(kc[None, :] - c[:, None] + NA_KW - 1, 0, 2 * NA_KW - 2)
    onehot = (off[None] == np.arange(2 * NA_KW - 1)[:, None, None]).astype(np.float32)
    tb = jnp.einsum("hdo,ock->hdck", rpb.astype(F32), jnp.asarray(onehot), precision=lax.Precision.HIGHEST)
    tb = jnp.where(inwin[None, None], tb, NEG)
    n_d = tb.shape[1] - (NA_PAIR - 1)
    pair = jnp.concatenate([tb[:, e:e + n_d] for e in range(NA_PAIR)], axis=-1)
    pair = pair.reshape(C_HEADS // 4, 4, n_d, GRID_W, LANES).transpose(0, 2, 1, 3, 4)
    return pair.reshape(C_HEADS // 4, n_d, 4 * GRID_W, LANES)


def _na_kernel(q_ref, k_ref, v_ref, tbl_ref, o_ref, *, rows):
    lane_head = _lane_head(GRID_W)

    def body(r, carry):
        rs = jnp.clip(r - NA_KH // 2, 0, rows - NA_KH)
        dr0 = rs - r + NA_KH - 1
        q0 = pl.multiple_of(r * GRID_W, GRID_W)
        k0 = pl.multiple_of(rs * GRID_W, GRID_W)
        qs = _stack_heads(q_ref[pl.ds(q0, GRID_W), :], lane_head)
        kw = k_ref[pl.ds(k0, NA_KEYS), :]
        vw = v_ref[pl.ds(k0, NA_KEYS), :]
        s = lax.dot_general(qs, kw, (((1,), (1,)), ((), ())), preferred_element_type=F32)
        s = s + jnp.concatenate([tbl_ref[dr0 + NA_PAIR * k] for k in range(NA_KH // NA_PAIR)], axis=1)
        m = jnp.max(s, axis=-1, keepdims=True)
        p = jnp.exp(s - m)
        l = jnp.sum(p, axis=-1, keepdims=True)
        pv = jnp.dot(p.astype(BF16), vw, preferred_element_type=F32) / l
        o = pv[0:GRID_W]
        for h in range(1, 4):
            o = jnp.where(lane_head == h, pv[h * GRID_W:(h + 1) * GRID_W], o)
        o_ref[pl.ds(q0, GRID_W), :] = o.astype(o_ref.dtype)
        return carry

    lax.fori_loop(0, rows, body, 0, unroll=NA_UNROLL)


def _mixer_na(proj3, tbl):
    bsz, t, _ = proj3.shape
    rows = t // GRID_W
    nhg = C_HEADS // 4

    def col_spec(off):
        return pl.BlockSpec((None, t, HEAD_GROUP_LANES), lambda hg, b: (b, 0, off + hg))

    return pl.pallas_call(
        functools.partial(_na_kernel, rows=rows),
        grid=(nhg, bsz),
        in_specs=[col_spec(COL_QC), col_spec(COL_KC), col_spec(COL_VC),
                  pl.BlockSpec((None,) + tbl.shape[1:], lambda hg, b: (hg, 0, 0, 0))],
        out_specs=pl.BlockSpec((None, t, HEAD_GROUP_LANES), lambda hg, b: (b, 0, hg)),
        out_shape=jax.ShapeDtypeStruct((bsz, t, C_WIDTH), BF16),
        compiler_params=_params(("parallel", "parallel")),
        name="na_attn",
    )(proj3, proj3, proj3, tbl)


def _tile(m, want):
    return want if m % want == 0 else m


TILE_PROJ = (1024, D_IN // 2)
TILE_MIX_OUT_ROWS = 512
TILE_SWIGLU = (2048, 512)
TILE_FFN_OUT = (1024, 512)
TILE_FFN_OUT_NORM = (512, 1024)


def _encode(x, w):
    bsz, t, d = x.shape
    m = bsz * t
    xf = x.reshape(m, d)
    for l in range(DEPTH):
        proj = _norm_proj(xf, w["norm1_g"][l], w["w_in"], l, _tile(m, TILE_PROJ[0]), TILE_PROJ[1])
        proj3 = proj.reshape(bsz, t, D_IN)
        o_a = _mixer_dilated(proj3)
        o_b = _mixer_rglru(proj3, w["rg_wg"][l], w["rg_pk"][l])
        o_c = _mixer_na(proj3, w["na_tbl"][l])
        xf, xn = _mix_out(xf, o_a.reshape(m, A_OUT), o_b.reshape(m, B_WIDTH), o_c.reshape(m, C_WIDTH),
                          w["w_out"], l, w["norm2_g"][l], _tile(m, TILE_MIX_OUT_ROWS))
        act = _swiglu(xn, w["w_ffn_in"], l, _tile(m, TILE_SWIGLU[0]), TILE_SWIGLU[1])
        if l < DEPTH - 1:
            xf = _ffn_out(xf, act, w["w_ffn_out"], l, _tile(m, TILE_FFN_OUT[0]), TILE_FFN_OUT[1])
        else:
            xf = _ffn_out_norm(xf, act, w["w_ffn_out"], l, w["final_g"], _tile(m, TILE_FFN_OUT_NORM[0]),
                               TILE_FFN_OUT_NORM[1])
    return xf.reshape(bsz, t, d)


def kernel(x_prompt, x_sample, norm1_g, w_in, conv_w, conv_b, rg_wa, rg_ba, rg_wx, rg_bx, rg_lam, na_rpb,
           w_out, norm2_g, w_ffn_in, w_ffn_out, final_g):
    rg = [_rglru_weights(conv_w[l], conv_b[l], rg_wa[l], rg_ba[l], rg_wx[l], rg_bx[l], rg_lam[l])
          for l in range(DEPTH)]
    w = {
        "norm1_g": norm1_g, "norm2_g": norm2_g, "final_g": final_g,
        "w_in": w_in.astype(BF16), "w_out": w_out.astype(BF16),
        "w_ffn_in": w_ffn_in.astype(BF16), "w_ffn_out": w_ffn_out.astype(BF16),
        "rg_wg": [r[0] for r in rg], "rg_pk": [r[1] for r in rg],
        "na_tbl": [_na_bias_tables(na_rpb[l]) for l in range(DEPTH)],
    }
    return _encode(x_prompt, w), _encode(x_sample, w)
```

```python
import functools

import numpy as np
import jax
import jax.numpy as jnp
from jax import lax
from jax.experimental import pallas as pl
from jax.experimental.pallas import tpu as pltpu

D_MODEL = 2048
DEPTH = 2
HEAD_DIM = 64
A_PATTERNS = ((128, 1), (512, 4), (2048, 16))
A_HEADS_PER_GROUP = 4
A_HEADS = A_HEADS_PER_GROUP * len(A_PATTERNS)
A_WIDTH = A_HEADS * HEAD_DIM
A_OUT = A_HEADS_PER_GROUP * HEAD_DIM
B_WIDTH = 512
B_BLOCKS = 8
CONV_W = 4
RG_C = 8.0
C_HEADS = 12
C_WIDTH = C_HEADS * HEAD_DIM
GRID_W = 64
NA_KH = 8
NA_KW = 16
D_FF = 5632
D_IN = 3 * A_WIDTH + 2 * B_WIDTH + 3 * C_WIDTH
D_MIX_OUT = A_OUT + B_WIDTH + C_WIDTH
NORM_EPS = 1e-6

F32 = jnp.float32
BF16 = jnp.bfloat16

HEAD_GROUP_LANES = 4 * HEAD_DIM
SCALE = HEAD_DIM ** -0.5
NEG = -1e30

COL_QA, COL_KA, COL_VA = 0, 3, 6
COL_GB, COL_XB = 9, 11
COL_QC, COL_KC, COL_VC = 13, 16, 19

MIB = 1024 * 1024
VMEM_LIMIT = 56 * MIB
VMEM_INTERNAL = 5 * MIB


def _params(sem, vmem_bytes=VMEM_LIMIT):
    return pltpu.CompilerParams(dimension_semantics=sem, vmem_limit_bytes=vmem_bytes)


def _rmsnorm_rows(x, g):
    ms = jnp.mean(x * x, axis=-1, keepdims=True)
    return x * lax.rsqrt(ms + NORM_EPS) * g


DENSE_ROWS = 256


def _row_chunks(n_rows):
    return [slice(r0, min(r0 + DENSE_ROWS, n_rows)) for r0 in range(0, n_rows, DENSE_ROWS)]


def _norm_proj_kernel(x_ref, g_ref, w_ref, o_ref, xn_ref):
    @pl.when(pl.program_id(1) == 0)
    def _():
        for rows in _row_chunks(x_ref.shape[0]):
            xn = _rmsnorm_rows(x_ref[rows, :], g_ref[...]).astype(BF16)
            xn_ref[rows, :] = xn
            o_ref[rows, :] = jnp.dot(xn, w_ref[...], preferred_element_type=F32).astype(o_ref.dtype)

    @pl.when(pl.program_id(1) != 0)
    def _():
        for rows in _row_chunks(x_ref.shape[0]):
            o_ref[rows, :] = jnp.dot(xn_ref[rows, :], w_ref[...], preferred_element_type=F32).astype(o_ref.dtype)


def _norm_proj(x, g, w, layer, tm, tn):
    m, d = x.shape
    n = w.shape[2]
    blocks = 2 * (tm * d * 4 + d * tn * 2 + tm * tn * 2) + tm * d * 2
    return pl.pallas_call(
        _norm_proj_kernel,
        grid=(m // tm, n // tn),
        in_specs=[
            pl.BlockSpec((tm, d), lambda i, j: (i, 0)),
            pl.BlockSpec((1, d), lambda i, j: (0, 0)),
            pl.BlockSpec((None, d, tn), lambda i, j: (layer, 0, j)),
        ],
        out_specs=pl.BlockSpec((tm, tn), lambda i, j: (i, j)),
        out_shape=jax.ShapeDtypeStruct((m, n), BF16),
        scratch_shapes=[pltpu.VMEM((tm, d), BF16)],
        compiler_params=_params(("parallel", "arbitrary"), blocks + VMEM_INTERNAL),
        name="norm_proj",
    )(x, g.reshape(1, d), w)


def _swiglu_kernel(xn_ref, wg_ref, wu_ref, o_ref):
    for rows in _row_chunks(xn_ref.shape[0]):
        xn = xn_ref[rows, :]
        gate = jnp.dot(xn, wg_ref[...], preferred_element_type=F32)
        up = jnp.dot(xn, wu_ref[...], preferred_element_type=F32)
        o_ref[rows, :] = (gate * jax.nn.sigmoid(gate) * up).astype(o_ref.dtype)


def _swiglu(xn, w, layer, tm, tn):
    m, d = xn.shape
    f = w.shape[2] // 2
    nf = f // tn
    return pl.pallas_call(
        _swiglu_kernel,
        grid=(m // tm, nf),
        in_specs=[
            pl.BlockSpec((tm, d), lambda i, j: (i, 0)),
            pl.BlockSpec((None, d, tn), lambda i, j: (layer, 0, j)),
            pl.BlockSpec((None, d, tn), lambda i, j: (layer, 0, j + nf)),
        ],
        out_specs=pl.BlockSpec((tm, tn), lambda i, j: (i, j)),
        out_shape=jax.ShapeDtypeStruct((m, f), BF16),
        compiler_params=_params(("parallel", "arbitrary")),
        name="swiglu",
    )(xn, w, w)


def _mix_out_kernel(res_ref, a_ref, b_ref, c_ref, w_ref, g_ref, o_ref, xn_ref):
    for rows in _row_chunks(res_ref.shape[0]):
        acc = res_ref[rows, :]
        acc = acc + jnp.dot(a_ref[rows, :], w_ref[0:A_OUT, :], preferred_element_type=F32)
        acc = acc + jnp.dot(b_ref[rows, :], w_ref[A_OUT:A_OUT + B_WIDTH, :], preferred_element_type=F32)
        acc = acc + jnp.dot(c_ref[rows, :], w_ref[A_OUT + B_WIDTH:, :], preferred_element_type=F32)
        o_ref[rows, :] = acc
        xn_ref[rows, :] = _rmsnorm_rows(acc, g_ref[...]).astype(BF16)


def _mix_out(res, o_a, o_b, o_c, w, layer, g, tm):
    m, d = res.shape
    row = lambda width: pl.BlockSpec((tm, width), lambda i: (i, 0))
    return pl.pallas_call(
        _mix_out_kernel,
        grid=(m // tm,),
        in_specs=[row(d), row(A_OUT), row(B_WIDTH), row(C_WIDTH),
                  pl.BlockSpec((None, D_MIX_OUT, d), lambda i: (layer, 0, 0)),
                  pl.BlockSpec((1, d), lambda i: (0, 0))],
        out_specs=(row(d), row(d)),
        out_shape=(jax.ShapeDtypeStruct((m, d), F32), jax.ShapeDtypeStruct((m, d), BF16)),
        compiler_params=_params(("parallel",)),
        name="mix_out",
    )(res, o_a, o_b, o_c, w, g.reshape(1, d))


def _ffn_out_kernel(res_ref, a_ref, w_ref, o_ref):
    o_ref[...] = res_ref[...] + jnp.dot(a_ref[...], w_ref[...], preferred_element_type=F32)


def _ffn_out(res, act, w, layer, tm, tn):
    m, d = res.shape
    f = act.shape[1]
    return pl.pallas_call(
        _ffn_out_kernel,
        grid=(m // tm, d // tn),
        in_specs=[
            pl.BlockSpec((tm, tn), lambda i, j: (i, j)),
            pl.BlockSpec((tm, f), lambda i, j: (i, 0)),
            pl.BlockSpec((None, f, tn), lambda i, j: (layer, 0, j)),
        ],
        out_specs=pl.BlockSpec((tm, tn), lambda i, j: (i, j)),
        out_shape=jax.ShapeDtypeStruct((m, d), F32),
        compiler_params=_params(("parallel", "arbitrary")),
        name="ffn_out",
    )(res, act, w)


def _ffn_out_norm_kernel(res_ref, a_ref, w_ref, g_ref, o_ref, *, tn):
    j = pl.program_id(1)
    cols = pl.ds(pl.multiple_of(j * tn, tn), tn)
    o_ref[:, cols] = res_ref[...] + jnp.dot(a_ref[...], w_ref[...], preferred_element_type=F32)

    @pl.when(j == pl.num_programs(1) - 1)
    def _():
        o_ref[...] = _rmsnorm_rows(o_ref[...], g_ref[...])


def _ffn_out_norm(res, act, w, layer, g, tm, tn):
    m, d = res.shape
    f = act.shape[1]
    return pl.pallas_call(
        functools.partial(_ffn_out_norm_kernel, tn=tn),
        grid=(m // tm, d // tn),
        in_specs=[
            pl.BlockSpec((tm, tn), lambda i, j: (i, j)),
            pl.BlockSpec((tm, f), lambda i, j: (i, 0)),
            pl.BlockSpec((None, f, tn), lambda i, j: (layer, 0, j)),
            pl.BlockSpec((1, d), lambda i, j: (0, 0)),
        ],
        out_specs=pl.BlockSpec((tm, d), lambda i, j: (i, 0)),
        out_shape=jax.ShapeDtypeStruct((m, d), F32),
        compiler_params=_params(("parallel", "arbitrary")),
        name="ffn_out_norm",
    )(res, act, w, g.reshape(1, d))


def _lane_head(rows):
    return lax.shift_right_logical(lax.broadcasted_iota(jnp.int32, (rows, HEAD_GROUP_LANES), 1), 6)


def _stack_heads(q, lane_head):
    qf = q.astype(F32) * SCALE
    return jnp.concatenate(
        [jnp.where(lane_head == h, qf, 0.0) for h in range(A_HEADS_PER_GROUP)], axis=0
    ).astype(BF16)


def _alibi_slope(h):
    return 2.0 ** (-8.0 * (h + 1) / A_HEADS)


LANES = 128
BF16_ROWS = 16
A_UNROLL = 8
A_COPY_ROWS = 256
A_CASES = 3


def _lane_halves(x):
    return [x[:, lh * LANES:(lh + 1) * LANES] for lh in range(HEAD_GROUP_LANES // LANES)]


def _dilated_group(gi, win, dil, t, q_ref, k_ref, v_ref, qd_ref, kd_ref, vd_ref, stage_ref, bias_ref,
                   m_ref, num_ref, den_ref):
    n = t // dil
    half = (win // 2) // dil
    tq = min(2 * half, n)
    nk = min(4 * half, n)
    tiles_per = n // tq
    n_halves = HEAD_GROUP_LANES // LANES

    if dil > 1:
        cr = min(A_COPY_ROWS, n)
        chunks_per = n // cr
        for src, dst in ((q_ref, qd_ref), (k_ref, kd_ref), (v_ref, vd_ref)):
            def stage(c, carry, src=src):
                c0 = pl.multiple_of(c * A_COPY_ROWS, A_COPY_ROWS)
                rows = _lane_halves(src[pl.ds(c0, A_COPY_ROWS), :].astype(F32))
                for lh in range(n_halves):
                    stage_ref[lh, pl.ds(c0, A_COPY_ROWS), :] = rows[lh]
                return carry

            lax.fori_loop(0, t // A_COPY_ROWS, stage, 0)

            def gather(c, carry, dst=dst):
                r = c // chunks_per
                u0 = pl.multiple_of((c % chunks_per) * cr, cr)
                rows = jnp.concatenate(
                    [stage_ref[lh, pl.ds(r + u0 * dil, cr, stride=dil), :] for lh in range(n_halves)], axis=1)
                dst[pl.ds(pl.multiple_of(r * n + u0, cr), cr), :] = rows.astype(BF16)
                return carry

            lax.fori_loop(0, dil * chunks_per, gather, 0)
        q_src, k_src, v_src = qd_ref, kd_ref, vd_ref
    else:
        q_src, k_src, v_src = q_ref, k_ref, v_ref

    row_iota = lax.broadcasted_iota(jnp.int32, (tq, nk), 0)
    col_iota = lax.broadcasted_iota(jnp.int32, (tq, nk), 1)
    for case in range(A_CASES):
        rel = jnp.abs(col_iota - case * half - row_iota)
        relf = rel.astype(F32)
        for h in range(A_HEADS_PER_GROUP):
            coef = _alibi_slope(gi * A_HEADS_PER_GROUP + h) * dil
            bias_ref[case * A_HEADS_PER_GROUP + h, 0:tq, 0:nk] = jnp.where(rel <= half, -coef * relf, NEG)

    lane_head = _lane_head(tq)

    def body(tile, carry):
        r = tile // tiles_per
        i = tile % tiles_per
        q0 = pl.multiple_of(i * tq, tq)
        start = pl.multiple_of(jnp.clip(q0 - half, 0, n - nk), half) if n > nk else 0
        case = (q0 - start) // half
        base = pl.multiple_of(r * n, tq)
        qs = _stack_heads(q_src[pl.ds(base + q0, tq), :], lane_head)
        k0 = pl.multiple_of(base + start, BF16_ROWS)
        kw = k_src[pl.ds(k0, nk), :]
        vw = v_src[pl.ds(k0, nk), :]
        s = lax.dot_general(qs, kw, (((1,), (1,)), ((), ())), preferred_element_type=F32)
        heads_per_half = LANES // HEAD_DIM
        first_head = lax.broadcasted_iota(jnp.int32, (tq, LANES), 1) < HEAD_DIM
        for lh in range(n_halves):
            parts = []
            for h in range(lh * heads_per_half, (lh + 1) * heads_per_half):
                sh = s[h * tq:(h + 1) * tq] + bias_ref[case * A_HEADS_PER_GROUP + h, 0:tq, 0:nk]
                m = jnp.max(sh, axis=-1, keepdims=True)
                p = jnp.exp(sh - m)
                l = jnp.sum(p, axis=-1, keepdims=True)
                pv = jnp.dot(p.astype(BF16), vw, preferred_element_type=F32)
                parts.append((jnp.broadcast_to(m, (tq, LANES)), jnp.broadcast_to(l, (tq, LANES)),
                              pv[:, lh * LANES:(lh + 1) * LANES]))
            (m0, l0, pv0), (m1, l1, pv1) = parts
            mx_h = jnp.where(first_head, m0, m1)
            den_h = jnp.where(first_head, l0, l1)
            num_h = jnp.where(first_head, pv0, pv1)
            if gi == 0:
                idx = pl.ds(q0, tq)
                m_ref[lh, idx, :] = mx_h
                num_ref[lh, idx, :] = num_h
                den_ref[lh, idx, :] = den_h
            else:
                idx = pl.ds(q0 * dil + r, tq, stride=dil)
                m_old = m_ref[lh, idx, :]
                m_new = jnp.maximum(m_old, mx_h)
                s_old = jnp.exp(m_old - m_new)
                s_new = jnp.exp(mx_h - m_new)
                num_ref[lh, idx, :] = num_ref[lh, idx, :] * s_old + num_h * s_new
                den_ref[lh, idx, :] = den_ref[lh, idx, :] * s_old + den_h * s_new
                m_ref[lh, idx, :] = m_new
        return carry

    lax.fori_loop(0, dil * tiles_per, body, 0, unroll=A_UNROLL)


def _dilated_kernel(q_ref, k_ref, v_ref, o_ref, qd_ref, kd_ref, vd_ref, stage_ref, bias_ref, m_ref, num_ref,
                    den_ref, *, t):
    g = pl.program_id(1)
    for gi, (win, dil) in enumerate(A_PATTERNS):
        @pl.when(g == gi)
        def _(gi=gi, win=win, dil=dil):
            _dilated_group(gi, win, dil, t, q_ref, k_ref, v_ref, qd_ref, kd_ref, vd_ref, stage_ref, bias_ref,
                           m_ref, num_ref, den_ref)

    @pl.when(g == len(A_PATTERNS) - 1)
    def _():
        def emit(c, carry):
            idx = pl.ds(pl.multiple_of(c * A_COPY_ROWS, A_COPY_ROWS), A_COPY_ROWS)
            o_ref[idx, :] = jnp.concatenate(
                [num_ref[lh, idx, :] / den_ref[lh, idx, :] for lh in range(HEAD_GROUP_LANES // LANES)],
                axis=1).astype(o_ref.dtype)
            return carry

        lax.fori_loop(0, t // A_COPY_ROWS, emit, 0)


def _mixer_dilated(proj3):
    bsz, t, _ = proj3.shape
    halves = {(win // 2) // dil for win, dil in A_PATTERNS}
    assert len(halves) == 1, "bias table scratch is sized for one band half-width"
    half = halves.pop()

    def col_spec(off):
        return pl.BlockSpec((None, t, HEAD_GROUP_LANES), lambda b, g: (b, 0, off + g))

    seq_bf16 = pltpu.VMEM((t, HEAD_GROUP_LANES), BF16)
    seq_halves = pltpu.VMEM((HEAD_GROUP_LANES // LANES, t, LANES), F32)
    return pl.pallas_call(
        functools.partial(_dilated_kernel, t=t),
        grid=(bsz, len(A_PATTERNS)),
        in_specs=[col_spec(COL_QA), col_spec(COL_KA), col_spec(COL_VA)],
        out_specs=pl.BlockSpec((None, t, A_OUT), lambda b, g: (b, 0, 0)),
        out_shape=jax.ShapeDtypeStruct((bsz, t, A_OUT), BF16),
        scratch_shapes=[seq_bf16, seq_bf16, seq_bf16, seq_halves,
                        pltpu.VMEM((A_CASES * A_HEADS_PER_GROUP, 2 * half, 4 * half), F32),
                        seq_halves, seq_halves, seq_halves],
        compiler_params=_params(("parallel", "arbitrary")),
        name="dilated_attn",
    )(proj3, proj3, proj3)


B_TILE = 256
B_CHUNK = 256
B_PAD = 8
LOG2_E = 1.4426950408889634
SUBLANES = 8
B_SCAN_UNROLL = 4


def _scan8(a, b, reverse):
    row = lax.broadcasted_iota(jnp.int32, a.shape, 0)
    for sh in (1, 2, 4):
        if reverse:
            a_s = pltpu.roll(a, SUBLANES - sh, 0)
            b_s = pltpu.roll(b, SUBLANES - sh, 0)
            ok = row < SUBLANES - sh
        else:
            a_s = pltpu.roll(a, sh, 0)
            b_s = pltpu.roll(b, sh, 0)
            ok = row >= sh
        a_s = jnp.where(ok, a_s, 1.0)
        b_s = jnp.where(ok, b_s, 0.0)
        b = a * b_s + b
        a = a * a_s
    return a, b


def _softplus(z):
    return jnp.maximum(z, 0.0) + jnp.log1p(jnp.exp(-jnp.abs(z)))


def _rglru_kernel(gb_ref, xb_ref, wg_ref, pk_ref, o_ref, xs_ref, af_ref, bf_ref, ab_ref, bb_ref, *, t):
    ct = B_TILE
    pk = pk_ref[...]
    conv_w = [pk[j:j + 1, :] for j in range(CONV_W)]
    conv_b = pk[4:5, :]
    bias = [pk[5 + j:6 + j, :] for j in range(4)]
    decay = [RG_C * _softplus(-pk[9 + j:10 + j, :]) for j in range(2)]
    decay_log2 = [-LOG2_E * dk for dk in decay]

    xs_ref[0:B_PAD, :] = jnp.zeros((B_PAD, ct), F32)
    xs_ref[B_PAD + t:2 * B_PAD + t, :] = jnp.zeros((B_PAD, ct), F32)

    def fill(c, carry):
        c0 = pl.multiple_of(c * B_CHUNK, B_CHUNK)
        xs_ref[pl.ds(c0 + B_PAD, B_CHUNK), :] = xb_ref[pl.ds(c0, B_CHUNK), :].astype(F32)
        return carry

    lax.fori_loop(0, t // B_CHUNK, fill, 0)

    def gates(c, carry):
        c0 = pl.multiple_of(c * B_CHUNK, B_CHUNK)
        slab = xs_ref[pl.ds(c0, B_CHUNK + 2 * B_PAD), :]
        left = CONV_W // 2
        xc = conv_b + conv_w[0] * slab[B_PAD - left:B_PAD - left + B_CHUNK]
        for j in range(1, CONV_W):
            xc = xc + conv_w[j] * slab[B_PAD - left + j:B_PAD - left + j + B_CHUNK]
        g = jnp.dot(xc.astype(BF16), wg_ref[...], preferred_element_type=F32)
        for d, (a_ref, b_ref) in enumerate(((af_ref, bf_ref), (ab_ref, bb_ref))):
            r = jax.nn.sigmoid(g[:, (2 * d) * ct:(2 * d + 1) * ct] + bias[2 * d])
            i = jax.nn.sigmoid(g[:, (2 * d + 1) * ct:(2 * d + 2) * ct] + bias[2 * d + 1])
            a = jnp.exp2(r * decay_log2[d])
            one_minus_a2 = jnp.tanh(r * decay[d]) * (a * a + 1.0)
            a_ref[pl.ds(c0, B_CHUNK), :] = a
            b_ref[pl.ds(c0, B_CHUNK), :] = jnp.sqrt(one_minus_a2) * (i * xc)
        return carry

    lax.fori_loop(0, t // B_CHUNK, gates, 0)

    def scan(k, carry):
        cf, cb = carry
        rf = pl.multiple_of(k * SUBLANES, SUBLANES)
        rb = pl.multiple_of(t - SUBLANES - k * SUBLANES, SUBLANES)
        a, b = _scan8(af_ref[pl.ds(rf, SUBLANES), :], bf_ref[pl.ds(rf, SUBLANES), :], False)
        hf = a * cf + b
        bf_ref[pl.ds(rf, SUBLANES), :] = hf
        a, b = _scan8(ab_ref[pl.ds(rb, SUBLANES), :], bb_ref[pl.ds(rb, SUBLANES), :], True)
        hb = a * cb + b
        bb_ref[pl.ds(rb, SUBLANES), :] = hb
        return hf[SUBLANES - 1:SUBLANES, :], hb[0:1, :]

    zero = jnp.zeros((1, ct), F32)
    lax.fori_loop(0, t // SUBLANES, scan, (zero, zero), unroll=B_SCAN_UNROLL)

    def emit(c, carry):
        c0 = pl.multiple_of(c * B_CHUNK, B_CHUNK)
        h = bf_ref[pl.ds(c0, B_CHUNK), :] + bb_ref[pl.ds(c0, B_CHUNK), :]
        gate = gb_ref[pl.ds(c0, B_CHUNK), :].astype(F32)
        o_ref[pl.ds(c0, B_CHUNK), :] = (jax.nn.gelu(gate) * h).astype(o_ref.dtype)
        return carry

    lax.fori_loop(0, t // B_CHUNK, emit, 0)


def _rglru_weights(conv_w, conv_b, wa, ba, wx, bx, lam):
    eye = jnp.eye(B_BLOCKS, dtype=F32)

    def dense(w):
        return jnp.einsum("ncd,nm->ncmd", w, eye).reshape(B_WIDTH, B_WIDTH)

    mats = [dense(wa[0]), dense(wx[0]), dense(wa[1]), dense(wx[1])]
    nt = B_WIDTH // B_TILE
    wg = jnp.stack([
        jnp.concatenate([m[c * B_TILE:(c + 1) * B_TILE, c * B_TILE:(c + 1) * B_TILE] for m in mats], axis=1)
        for c in range(nt)
    ]).astype(BF16)
    rows = [conv_w[j] for j in range(CONV_W)] + [conv_b, ba[0], bx[0], ba[1], bx[1], lam[0], lam[1]]
    rows += [jnp.zeros_like(conv_b)] * (16 - len(rows))
    pk = jnp.stack(rows).reshape(16, nt, B_TILE).transpose(1, 0, 2)
    return wg, pk


def _mixer_rglru(proj3, wg, pk):
    bsz, t, _ = proj3.shape
    nt = B_WIDTH // B_TILE
    return pl.pallas_call(
        functools.partial(_rglru_kernel, t=t),
        grid=(bsz, nt),
        in_specs=[
            pl.BlockSpec((None, t, B_TILE), lambda b, c: (b, 0, COL_GB + c)),
            pl.BlockSpec((None, t, B_TILE), lambda b, c: (b, 0, COL_XB + c)),
            pl.BlockSpec((None, B_TILE, 4 * B_TILE), lambda b, c: (c, 0, 0)),
            pl.BlockSpec((None, 16, B_TILE), lambda b, c: (c, 0, 0)),
        ],
        out_specs=pl.BlockSpec((None, t, B_TILE), lambda b, c: (b, 0, c)),
        out_shape=jax.ShapeDtypeStruct((bsz, t, B_WIDTH), BF16),
        scratch_shapes=[pltpu.VMEM((t + 2 * B_PAD, B_TILE), F32)] + [pltpu.VMEM((t, B_TILE), F32)] * 4,
        compiler_params=_params(("parallel", "parallel")),
        name="rglru",
    )(proj3, proj3, wg, pk)


NA_KEYS = NA_KH * GRID_W
NA_UNROLL = 16


NA_PAIR = LANES // GRID_W


def _na_bias_tables(rpb):
    c = np.arange(GRID_W)
    cs = np.clip(c - NA_KW // 2, 0, GRID_W - NA_KW)
    kc = np.arange(GRID_W)
    inwin = (kc[None, :] >= cs[:, None]) & (kc[None, :] < cs[:, None] + NA_KW)
    off = np.clip(kc[None, :] - c[:, None] + NA_KW - 1, 0, 2 * NA_KW - 2)
    onehot = (off[None] == np.arange(2 * NA_KW - 1)[:, None, None]).astype(np.float32)
    tb = jnp.einsum("hdo,ock->hdck", rpb.astype(F32), jnp.asarray(onehot), precision=lax.Precision.HIGHEST)
    tb = jnp.where(inwin[None, None], tb, NEG)
    n_d = tb.shape[1] - (NA_PAIR - 1)
    pair = jnp.concatenate([tb[:, e:e + n_d] for e in range(NA_PAIR)], axis=-1)
    pair = pair.reshape(C_HEADS // 4, 4, n_d, GRID_W, LANES).transpose(0, 2, 1, 3, 4)
    return pair.reshape(C_HEADS // 4, n_d, 4 * GRID_W, LANES)


def _na_kernel(q_ref, k_ref, v_ref, tbl_ref, o_ref, *, rows):
    lane_head = _lane_head(GRID_W)

    def body(r, carry):
        rs = jnp.clip(r - NA_KH // 2, 0, rows - NA_KH)
        dr0 = rs - r + NA_KH - 1
        q0 = pl.multiple_of(r * GRID_W, GRID_W)
        k0 = pl.multiple_of(rs * GRID_W, GRID_W)
        qs = _stack_heads(q_ref[pl.ds(q0, GRID_W), :], lane_head)
        kw = k_ref[pl.ds(k0, NA_KEYS), :]
        vw = v_ref[pl.ds(k0, NA_KEYS), :]
        s = lax.dot_general(qs, kw, (((1,), (1,)), ((), ())), preferred_element_type=F32)
        s = s + jnp.concatenate([tbl_ref[dr0 + NA_PAIR * k] for k in range(NA_KH // NA_PAIR)], axis=1)
        m = jnp.max(s, axis=-1, keepdims=True)
        p = jnp.exp(s - m)
        l = jnp.sum(p, axis=-1, keepdims=True)
        pv = jnp.dot(p.astype(BF16), vw, preferred_element_type=F32) / l
        o = pv[0:GRID_W]
        for h in range(1, 4):
            o = jnp.where(lane_head == h, pv[h * GRID_W:(h + 1) * GRID_W], o)
        o_ref[pl.ds(q0, GRID_W), :] = o.astype(o_ref.dtype)
        return carry

    lax.fori_loop(0, rows, body, 0, unroll=NA_UNROLL)


def _mixer_na(proj3, tbl):
    bsz, t, _ = proj3.shape
    rows = t // GRID_W
    nhg = C_HEADS // 4

    def col_spec(off):
        return pl.BlockSpec((None, t, HEAD_GROUP_LANES), lambda hg, b: (b, 0, off + hg))

    return pl.pallas_call(
        functools.partial(_na_kernel, rows=rows),
        grid=(nhg, bsz),
        in_specs=[col_spec(COL_QC), col_spec(COL_KC), col_spec(COL_VC),
                  pl.BlockSpec((None,) + tbl.shape[1:], lambda hg, b: (hg, 0, 0, 0))],
        out_specs=pl.BlockSpec((None, t, HEAD_GROUP_LANES), lambda hg, b: (b, 0, hg)),
        out_shape=jax.ShapeDtypeStruct((bsz, t, C_WIDTH), BF16),
        compiler_params=_params(("parallel", "parallel")),
        name="na_attn",
    )(proj3, proj3, proj3, tbl)


def _tile(m, want):
    return want if m % want == 0 else m


TILE_PROJ = (1024, D_IN // 2)
TILE_MIX_OUT_ROWS = 512
TILE_SWIGLU = (4096, 512)
TILE_FFN_OUT = (1024, 512)
TILE_FFN_OUT_NORM = (512, 1024)


def _encode(x, w):
    bsz, t, d = x.shape
    m = bsz * t
    xf = x.reshape(m, d)
    for l in range(DEPTH):
        proj = _norm_proj(xf, w["norm1_g"][l], w["w_in"], l, _tile(m, TILE_PROJ[0]), TILE_PROJ[1])
        proj3 = proj.reshape(bsz, t, D_IN)
        o_a = _mixer_dilated(proj3)
        o_b = _mixer_rglru(proj3, w["rg_wg"][l], w["rg_pk"][l])
        o_c = _mixer_na(proj3, w["na_tbl"][l])
        xf, xn = _mix_out(xf, o_a.reshape(m, A_OUT), o_b.reshape(m, B_WIDTH), o_c.reshape(m, C_WIDTH),
                          w["w_out"], l, w["norm2_g"][l], _tile(m, TILE_MIX_OUT_ROWS))
        act = _swiglu(xn, w["w_ffn_in"], l, _tile(m, TILE_SWIGLU[0]), TILE_SWIGLU[1])
        if l < DEPTH - 1:
            xf = _ffn_out(xf, act, w["w_ffn_out"], l, _tile(m, TILE_FFN_OUT[0]), TILE_FFN_OUT[1])
        else:
            xf = _ffn_out_norm(xf, act, w["w_ffn_out"], l, w["final_g"], _tile(m, TILE_FFN_OUT_NORM[0]),
                               TILE_FFN_OUT_NORM[1])
    return xf.reshape(bsz, t, d)


def kernel(x_prompt, x_sample, norm1_g, w_in, conv_w, conv_b, rg_wa, rg_ba, rg_wx, rg_bx, rg_lam, na_rpb,
           w_out, norm2_g, w_ffn_in, w_ffn_out, final_g):
    rg = [_rglru_weights(conv_w[l], conv_b[l], rg_wa[l], rg_ba[l], rg_wx[l], rg_bx[l], rg_lam[l])
          for l in range(DEPTH)]
    w = {
        "norm1_g": norm1_g, "norm2_g": norm2_g, "final_g": final_g,
        "w_in": w_in.astype(BF16), "w_out": w_out.astype(BF16),
        "w_ffn_in": w_ffn_in.astype(BF16), "w_ffn_out": w_ffn_out.astype(BF16),
        "rg_wg": [r[0] for r in rg], "rg_pk": [r[1] for r in rg],
        "na_tbl": [_na_bias_tables(na_rpb[l]) for l in range(DEPTH)],
    }
    return _encode(x_prompt, w), _encode(x_sample, w)
```

```python
import functools

import numpy as np
import jax
import jax.numpy as jnp
from jax import lax
from jax.experimental import pallas as pl
from jax.experimental.pallas import tpu as pltpu

D_MODEL = 2048
DEPTH = 2
HEAD_DIM = 64
A_PATTERNS = ((128, 1), (512, 4), (2048, 16))
A_HEADS_PER_GROUP = 4
A_HEADS = A_HEADS_PER_GROUP * len(A_PATTERNS)
A_WIDTH = A_HEADS * HEAD_DIM
A_OUT = A_HEADS_PER_GROUP * HEAD_DIM
B_WIDTH = 512
B_BLOCKS = 8
CONV_W = 4
RG_C = 8.0
C_HEADS = 12
C_WIDTH = C_HEADS * HEAD_DIM
GRID_W = 64
NA_KH = 8
NA_KW = 16
D_FF = 5632
D_IN = 3 * A_WIDTH + 2 * B_WIDTH + 3 * C_WIDTH
D_MIX_OUT = A_OUT + B_WIDTH + C_WIDTH
NORM_EPS = 1e-6

F32 = jnp.float32
BF16 = jnp.bfloat16

HEAD_GROUP_LANES = 4 * HEAD_DIM
SCALE = HEAD_DIM ** -0.5
NEG = -1e30

COL_QA, COL_KA, COL_VA = 0, 3, 6
COL_GB, COL_XB = 9, 11
COL_QC, COL_KC, COL_VC = 13, 16, 19

MIB = 1024 * 1024
VMEM_LIMIT = 56 * MIB
VMEM_INTERNAL = 5 * MIB


def _params(sem, vmem_bytes=VMEM_LIMIT):
    return pltpu.CompilerParams(dimension_semantics=sem, vmem_limit_bytes=vmem_bytes)


def _rmsnorm_rows(x, g):
    ms = jnp.mean(x * x, axis=-1, keepdims=True)
    return x * lax.rsqrt(ms + NORM_EPS) * g


DENSE_ROWS = 256


def _row_chunks(n_rows):
    return [slice(r0, min(r0 + DENSE_ROWS, n_rows)) for r0 in range(0, n_rows, DENSE_ROWS)]


def _norm_proj_kernel(x_ref, g_ref, w_ref, o_ref, xn_ref):
    @pl.when(pl.program_id(1) == 0)
    def _():
        for rows in _row_chunks(x_ref.shape[0]):
            xn = _rmsnorm_rows(x_ref[rows, :], g_ref[...]).astype(BF16)
            xn_ref[rows, :] = xn
            o_ref[rows, :] = jnp.dot(xn, w_ref[...], preferred_element_type=F32).astype(o_ref.dtype)

    @pl.when(pl.program_id(1) != 0)
    def _():
        for rows in _row_chunks(x_ref.shape[0]):
            o_ref[rows, :] = jnp.dot(xn_ref[rows, :], w_ref[...], preferred_element_type=F32).astype(o_ref.dtype)


def _norm_proj(x, g, w, layer, tm, tn):
    m, d = x.shape
    n = w.shape[2]
    blocks = 2 * (tm * d * 4 + d * tn * 2 + tm * tn * 2) + tm * d * 2
    return pl.pallas_call(
        _norm_proj_kernel,
        grid=(m // tm, n // tn),
        in_specs=[
            pl.BlockSpec((tm, d), lambda i, j: (i, 0)),
            pl.BlockSpec((1, d), lambda i, j: (0, 0)),
            pl.BlockSpec((None, d, tn), lambda i, j: (layer, 0, j)),
        ],
        out_specs=pl.BlockSpec((tm, tn), lambda i, j: (i, j)),
        out_shape=jax.ShapeDtypeStruct((m, n), BF16),
        scratch_shapes=[pltpu.VMEM((tm, d), BF16)],
        compiler_params=_params(("parallel", "arbitrary"), blocks + VMEM_INTERNAL),
        name="norm_proj",
    )(x, g.reshape(1, d), w)


def _swiglu_kernel(xn_ref, wg_ref, wu_ref, o_ref):
    for rows in _row_chunks(xn_ref.shape[0]):
        xn = xn_ref[rows, :]
        gate = jnp.dot(xn, wg_ref[...], preferred_element_type=F32)
        up = jnp.dot(xn, wu_ref[...], preferred_element_type=F32)
        o_ref[rows, :] = (gate * jax.nn.sigmoid(gate) * up).astype(o_ref.dtype)


def _swiglu(xn, w, layer, tm, tn):
    m, d = xn.shape
    f = w.shape[2] // 2
    nf = f // tn
    return pl.pallas_call(
        _swiglu_kernel,
        grid=(m // tm, nf),
        in_specs=[
            pl.BlockSpec((tm, d), lambda i, j: (i, 0)),
            pl.BlockSpec((None, d, tn), lambda i, j: (layer, 0, j)),
            pl.BlockSpec((None, d, tn), lambda i, j: (layer, 0, j + nf)),
        ],
        out_specs=pl.BlockSpec((tm, tn), lambda i, j: (i, j)),
        out_shape=jax.ShapeDtypeStruct((m, f), BF16),
        compiler_params=_params(("parallel", "arbitrary")),
        name="swiglu",
    )(xn, w, w)


def _mix_out_kernel(res_ref, a_ref, b_ref, c_ref, w_ref, g_ref, o_ref, xn_ref):
    for rows in _row_chunks(res_ref.shape[0]):
        acc = res_ref[rows, :]
        acc = acc + jnp.dot(a_ref[rows, :], w_ref[0:A_OUT, :], preferred_element_type=F32)
        acc = acc + jnp.dot(b_ref[rows, :], w_ref[A_OUT:A_OUT + B_WIDTH, :], preferred_element_type=F32)
        acc = acc + jnp.dot(c_ref[rows, :], w_ref[A_OUT + B_WIDTH:, :], preferred_element_type=F32)
        o_ref[rows, :] = acc
        xn_ref[rows, :] = _rmsnorm_rows(acc, g_ref[...]).astype(BF16)


def _mix_out(res, o_a, o_b, o_c, w, layer, g, tm):
    m, d = res.shape
    row = lambda width: pl.BlockSpec((tm, width), lambda i: (i, 0))
    return pl.pallas_call(
        _mix_out_kernel,
        grid=(m // tm,),
        in_specs=[row(d), row(A_OUT), row(B_WIDTH), row(C_WIDTH),
                  pl.BlockSpec((None, D_MIX_OUT, d), lambda i: (layer, 0, 0)),
                  pl.BlockSpec((1, d), lambda i: (0, 0))],
        out_specs=(row(d), row(d)),
        out_shape=(jax.ShapeDtypeStruct((m, d), F32), jax.ShapeDtypeStruct((m, d), BF16)),
        compiler_params=_params(("parallel",)),
        name="mix_out",
    )(res, o_a, o_b, o_c, w, g.reshape(1, d))


def _ffn_out_kernel(res_ref, a_ref, w_ref, o_ref):
    o_ref[...] = res_ref[...] + jnp.dot(a_ref[...], w_ref[...], preferred_element_type=F32)


def _ffn_out(res, act, w, layer, tm, tn):
    m, d = res.shape
    f = act.shape[1]
    return pl.pallas_call(
        _ffn_out_kernel,
        grid=(m // tm, d // tn),
        in_specs=[
            pl.BlockSpec((tm, tn), lambda i, j: (i, j)),
            pl.BlockSpec((tm, f), lambda i, j: (i, 0)),
            pl.BlockSpec((None, f, tn), lambda i, j: (layer, 0, j)),
        ],
        out_specs=pl.BlockSpec((tm, tn), lambda i, j: (i, j)),
        out_shape=jax.ShapeDtypeStruct((m, d), F32),
        compiler_params=_params(("parallel", "arbitrary")),
        name="ffn_out",
    )(res, act, w)


def _ffn_out_norm_kernel(res_ref, a_ref, w_ref, g_ref, o_ref, *, tn):
    j = pl.program_id(1)
    cols = pl.ds(pl.multiple_of(j * tn, tn), tn)
    o_ref[:, cols] = res_ref[...] + jnp.dot(a_ref[...], w_ref[...], preferred_element_type=F32)

    @pl.when(j == pl.num_programs(1) - 1)
    def _():
        o_ref[...] = _rmsnorm_rows(o_ref[...], g_ref[...])


def _ffn_out_norm(res, act, w, layer, g, tm, tn):
    m, d = res.shape
    f = act.shape[1]
    return pl.pallas_call(
        functools.partial(_ffn_out_norm_kernel, tn=tn),
        grid=(m // tm, d // tn),
        in_specs=[
            pl.BlockSpec((tm, tn), lambda i, j: (i, j)),
            pl.BlockSpec((tm, f), lambda i, j: (i, 0)),
            pl.BlockSpec((None, f, tn), lambda i, j: (layer, 0, j)),
            pl.BlockSpec((1, d), lambda i, j: (0, 0)),
        ],
        out_specs=pl.BlockSpec((tm, d), lambda i, j: (i, 0)),
        out_shape=jax.ShapeDtypeStruct((m, d), F32),
        compiler_params=_params(("parallel", "arbitrary")),
        name="ffn_out_norm",
    )(res, act, w, g.reshape(1, d))


def _lane_head(rows):
    return lax.shift_right_logical(lax.broadcasted_iota(jnp.int32, (rows, HEAD_GROUP_LANES), 1), 6)


def _stack_heads(q, lane_head):
    qf = q.astype(F32) * SCALE
    return jnp.concatenate(
        [jnp.where(lane_head == h, qf, 0.0) for h in range(A_HEADS_PER_GROUP)], axis=0
    ).astype(BF16)


def _alibi_slope(h):
    return 2.0 ** (-8.0 * (h + 1) / A_HEADS)


LANES = 128
BF16_ROWS = 16
A_UNROLL = 8
A_COPY_ROWS = 256
A_CASES = 3


def _lane_halves(x):
    return [x[:, lh * LANES:(lh + 1) * LANES] for lh in range(HEAD_GROUP_LANES // LANES)]


def _dilated_group(gi, win, dil, t, q_ref, k_ref, v_ref, qd_ref, kd_ref, vd_ref, stage_ref, bias_ref,
                   m_ref, num_ref, den_ref):
    n = t // dil
    half = (win // 2) // dil
    tq = min(2 * half, n)
    nk = min(4 * half, n)
    tiles_per = n // tq
    n_halves = HEAD_GROUP_LANES // LANES

    if dil > 1:
        cr = min(A_COPY_ROWS, n)
        chunks_per = n // cr
        for src, dst in ((q_ref, qd_ref), (k_ref, kd_ref), (v_ref, vd_ref)):
            def stage(c, carry, src=src):
                c0 = pl.multiple_of(c * A_COPY_ROWS, A_COPY_ROWS)
                rows = _lane_halves(src[pl.ds(c0, A_COPY_ROWS), :].astype(F32))
                for lh in range(n_halves):
                    stage_ref[lh, pl.ds(c0, A_COPY_ROWS), :] = rows[lh]
                return carry

            lax.fori_loop(0, t // A_COPY_ROWS, stage, 0)

            def gather(c, carry, dst=dst):
                r = c // chunks_per
                u0 = pl.multiple_of((c % chunks_per) * cr, cr)
                rows = jnp.concatenate(
                    [stage_ref[lh, pl.ds(r + u0 * dil, cr, stride=dil), :] for lh in range(n_halves)], axis=1)
                dst[pl.ds(pl.multiple_of(r * n + u0, cr), cr), :] = rows.astype(BF16)
                return carry

            lax.fori_loop(0, dil * chunks_per, gather, 0)
        q_src, k_src, v_src = qd_ref, kd_ref, vd_ref
    else:
        q_src, k_src, v_src = q_ref, k_ref, v_ref

    row_iota = lax.broadcasted_iota(jnp.int32, (tq, nk), 0)
    col_iota = lax.broadcasted_iota(jnp.int32, (tq, nk), 1)
    for case in range(A_CASES):
        rel = jnp.abs(col_iota - case * half - row_iota)
        relf = rel.astype(F32)
        for h in range(A_HEADS_PER_GROUP):
            coef = _alibi_slope(gi * A_HEADS_PER_GROUP + h) * dil
            bias_ref[case * A_HEADS_PER_GROUP + h, 0:tq, 0:nk] = jnp.where(rel <= half, -coef * relf, NEG)

    lane_head = _lane_head(tq)

    def body(tile, carry):
        r = tile // tiles_per
        i = tile % tiles_per
        q0 = pl.multiple_of(i * tq, tq)
        start = pl.multiple_of(jnp.clip(q0 - half, 0, n - nk), half) if n > nk else 0
        case = (q0 - start) // half
        base = pl.multiple_of(r * n, tq)
        qs = _stack_heads(q_src[pl.ds(base + q0, tq), :], lane_head)
        k0 = pl.multiple_of(base + start, BF16_ROWS)
        kw = k_src[pl.ds(k0, nk), :]
        vw = v_src[pl.ds(k0, nk), :]
        s = lax.dot_general(qs, kw, (((1,), (1,)), ((), ())), preferred_element_type=F32)
        heads_per_half = LANES // HEAD_DIM
        first_head = lax.broadcasted_iota(jnp.int32, (tq, LANES), 1) < HEAD_DIM
        for lh in range(n_halves):
            parts = []
            for h in range(lh * heads_per_half, (lh + 1) * heads_per_half):
                sh = s[h * tq:(h + 1) * tq] + bias_ref[case * A_HEADS_PER_GROUP + h, 0:tq, 0:nk]
                m = jnp.max(sh, axis=-1, keepdims=True)
                p = jnp.exp(sh - m)
                l = jnp.sum(p, axis=-1, keepdims=True)
                pv = jnp.dot(p.astype(BF16), vw, preferred_element_type=F32)
                parts.append((jnp.broadcast_to(m, (tq, LANES)), jnp.broadcast_to(l, (tq, LANES)),
                              pv[:, lh * LANES:(lh + 1) * LANES]))
            (m0, l0, pv0), (m1, l1, pv1) = parts
            mx_h = jnp.where(first_head, m0, m1)
            den_h = jnp.where(first_head, l0, l1)
            num_h = jnp.where(first_head, pv0, pv1)
            if gi == 0:
                idx = pl.ds(q0, tq)
                m_ref[lh, idx, :] = mx_h
                num_ref[lh, idx, :] = num_h
                den_ref[lh, idx, :] = den_h
            else:
                idx = pl.ds(q0 * dil + r, tq, stride=dil)
                m_old = m_ref[lh, idx, :]
                m_new = jnp.maximum(m_old, mx_h)
                s_old = jnp.exp(m_old - m_new)
                s_new = jnp.exp(mx_h - m_new)
                num_ref[lh, idx, :] = num_ref[lh, idx, :] * s_old + num_h * s_new
                den_ref[lh, idx, :] = den_ref[lh, idx, :] * s_old + den_h * s_new
                m_ref[lh, idx, :] = m_new
        return carry

    lax.fori_loop(0, dil * tiles_per, body, 0, unroll=A_UNROLL)


def _dilated_kernel(q_ref, k_ref, v_ref, o_ref, qd_ref, kd_ref, vd_ref, stage_ref, bias_ref, m_ref, num_ref,
                    den_ref, *, t):
    g = pl.program_id(1)
    for gi, (win, dil) in enumerate(A_PATTERNS):
        @pl.when(g == gi)
        def _(gi=gi, win=win, dil=dil):
            _dilated_group(gi, win, dil, t, q_ref, k_ref, v_ref, qd_ref, kd_ref, vd_ref, stage_ref, bias_ref,
                           m_ref, num_ref, den_ref)

    @pl.when(g == len(A_PATTERNS) - 1)
    def _():
        def emit(c, carry):
            idx = pl.ds(pl.multiple_of(c * A_COPY_ROWS, A_COPY_ROWS), A_COPY_ROWS)
            o_ref[idx, :] = jnp.concatenate(
                [num_ref[lh, idx, :] / den_ref[lh, idx, :] for lh in range(HEAD_GROUP_LANES // LANES)],
                axis=1).astype(o_ref.dtype)
            return carry

        lax.fori_loop(0, t // A_COPY_ROWS, emit, 0)


def _mixer_dilated(proj3):
    bsz, t, _ = proj3.shape
    halves = {(win // 2) // dil for win, dil in A_PATTERNS}
    assert len(halves) == 1, "bias table scratch is sized for one band half-width"
    half = halves.pop()

    def col_spec(off):
        return pl.BlockSpec((None, t, HEAD_GROUP_LANES), lambda b, g: (b, 0, off + g))

    seq_bf16 = pltpu.VMEM((t, HEAD_GROUP_LANES), BF16)
    seq_halves = pltpu.VMEM((HEAD_GROUP_LANES // LANES, t, LANES), F32)
    return pl.pallas_call(
        functools.partial(_dilated_kernel, t=t),
        grid=(bsz, len(A_PATTERNS)),
        in_specs=[col_spec(COL_QA), col_spec(COL_KA), col_spec(COL_VA)],
        out_specs=pl.BlockSpec((None, t, A_OUT), lambda b, g: (b, 0, 0)),
        out_shape=jax.ShapeDtypeStruct((bsz, t, A_OUT), BF16),
        scratch_shapes=[seq_bf16, seq_bf16, seq_bf16, seq_halves,
                        pltpu.VMEM((A_CASES * A_HEADS_PER_GROUP, 2 * half, 4 * half), F32),
                        seq_halves, seq_halves, seq_halves],
        compiler_params=_params(("parallel", "arbitrary")),
        name="dilated_attn",
    )(proj3, proj3, proj3)


B_TILE = 256
B_CHUNK = 256
B_PAD = 8
LOG2_E = 1.4426950408889634
SUBLANES = 8
B_SCAN_UNROLL = 4


def _scan8(a, b, reverse):
    row = lax.broadcasted_iota(jnp.int32, a.shape, 0)
    for sh in (1, 2, 4):
        if reverse:
            a_s = pltpu.roll(a, SUBLANES - sh, 0)
            b_s = pltpu.roll(b, SUBLANES - sh, 0)
            ok = row < SUBLANES - sh
        else:
            a_s = pltpu.roll(a, sh, 0)
            b_s = pltpu.roll(b, sh, 0)
            ok = row >= sh
        a_s = jnp.where(ok, a_s, 1.0)
        b_s = jnp.where(ok, b_s, 0.0)
        b = a * b_s + b
        a = a * a_s
    return a, b


def _softplus(z):
    return jnp.maximum(z, 0.0) + jnp.log1p(jnp.exp(-jnp.abs(z)))


def _rglru_kernel(gb_ref, xb_ref, wg_ref, pk_ref, o_ref, xs_ref, af_ref, bf_ref, ab_ref, bb_ref, *, t):
    ct = B_TILE
    pk = pk_ref[...]
    conv_w = [pk[j:j + 1, :] for j in range(CONV_W)]
    conv_b = pk[4:5, :]
    bias = [pk[5 + j:6 + j, :] for j in range(4)]
    decay = [RG_C * _softplus(-pk[9 + j:10 + j, :]) for j in range(2)]
    decay_log2 = [-LOG2_E * dk for dk in decay]

    xs_ref[0:B_PAD, :] = jnp.zeros((B_PAD, ct), F32)
    xs_ref[B_PAD + t:2 * B_PAD + t, :] = jnp.zeros((B_PAD, ct), F32)

    def fill(c, carry):
        c0 = pl.multiple_of(c * B_CHUNK, B_CHUNK)
        xs_ref[pl.ds(c0 + B_PAD, B_CHUNK), :] = xb_ref[pl.ds(c0, B_CHUNK), :].astype(F32)
        return carry

    lax.fori_loop(0, t // B_CHUNK, fill, 0)

    def gates(c, carry):
        c0 = pl.multiple_of(c * B_CHUNK, B_CHUNK)
        slab = xs_ref[pl.ds(c0, B_CHUNK + 2 * B_PAD), :]
        left = CONV_W // 2
        xc = conv_b + conv_w[0] * slab[B_PAD - left:B_PAD - left + B_CHUNK]
        for j in range(1, CONV_W):
            xc = xc + conv_w[j] * slab[B_PAD - left + j:B_PAD - left + j + B_CHUNK]
        g = jnp.dot(xc.astype(BF16), wg_ref[...], preferred_element_type=F32)
        for d, (a_ref, b_ref) in enumerate(((af_ref, bf_ref), (ab_ref, bb_ref))):
            r = jax.nn.sigmoid(g[:, (2 * d) * ct:(2 * d + 1) * ct] + bias[2 * d])
            i = jax.nn.sigmoid(g[:, (2 * d + 1) * ct:(2 * d + 2) * ct] + bias[2 * d + 1])
            a = jnp.exp2(r * decay_log2[d])
            one_minus_a2 = jnp.tanh(r * decay[d]) * (a * a + 1.0)
            a_ref[pl.ds(c0, B_CHUNK), :] = a
            b_ref[pl.ds(c0, B_CHUNK), :] = jnp.sqrt(one_minus_a2) * (i * xc)
        return carry

    lax.fori_loop(0, t // B_CHUNK, gates, 0)

    def scan(k, carry):
        cf, cb = carry
        rf = pl.multiple_of(k * SUBLANES, SUBLANES)
        rb = pl.multiple_of(t - SUBLANES - k * SUBLANES, SUBLANES)
        a, b = _scan8(af_ref[pl.ds(rf, SUBLANES), :], bf_ref[pl.ds(rf, SUBLANES), :], False)
        hf = a * cf + b
        bf_ref[pl.ds(rf, SUBLANES), :] = hf
        a, b = _scan8(ab_ref[pl.ds(rb, SUBLANES), :], bb_ref[pl.ds(rb, SUBLANES), :], True)
        hb = a * cb + b
        bb_ref[pl.ds(rb, SUBLANES), :] = hb
        return hf[SUBLANES - 1:SUBLANES, :], hb[0:1, :]

    zero = jnp.zeros((1, ct), F32)
    lax.fori_loop(0, t // SUBLANES, scan, (zero, zero), unroll=B_SCAN_UNROLL)

    def emit(c, carry):
        c0 = pl.multiple_of(c * B_CHUNK, B_CHUNK)
        h = bf_ref[pl.ds(c0, B_CHUNK), :] + bb_ref[pl.ds(c0, B_CHUNK), :]
        gate = gb_ref[pl.ds(c0, B_CHUNK), :].astype(F32)
        o_ref[pl.ds(c0, B_CHUNK), :] = (jax.nn.gelu(gate) * h).astype(o_ref.dtype)
        return carry

    lax.fori_loop(0, t // B_CHUNK, emit, 0)


def _rglru_weights(conv_w, conv_b, wa, ba, wx, bx, lam):
    eye = jnp.eye(B_BLOCKS, dtype=F32)

    def dense(w):
        return jnp.einsum("ncd,nm->ncmd", w, eye).reshape(B_WIDTH, B_WIDTH)

    mats = [dense(wa[0]), dense(wx[0]), dense(wa[1]), dense(wx[1])]
    nt = B_WIDTH // B_TILE
    wg = jnp.stack([
        jnp.concatenate([m[c * B_TILE:(c + 1) * B_TILE, c * B_TILE:(c + 1) * B_TILE] for m in mats], axis=1)
        for c in range(nt)
    ]).astype(BF16)
    rows = [conv_w[j] for j in range(CONV_W)] + [conv_b, ba[0], bx[0], ba[1], bx[1], lam[0], lam[1]]
    rows += [jnp.zeros_like(conv_b)] * (16 - len(rows))
    pk = jnp.stack(rows).reshape(16, nt, B_TILE).transpose(1, 0, 2)
    return wg, pk


def _mixer_rglru(proj3, wg, pk):
    bsz, t, _ = proj3.shape
    nt = B_WIDTH // B_TILE
    return pl.pallas_call(
        functools.partial(_rglru_kernel, t=t),
        grid=(bsz, nt),
        in_specs=[
            pl.BlockSpec((None, t, B_TILE), lambda b, c: (b, 0, COL_GB + c)),
            pl.BlockSpec((None, t, B_TILE), lambda b, c: (b, 0, COL_XB + c)),
            pl.BlockSpec((None, B_TILE, 4 * B_TILE), lambda b, c: (c, 0, 0)),
            pl.BlockSpec((None, 16, B_TILE), lambda b, c: (c, 0, 0)),
        ],
        out_specs=pl.BlockSpec((None, t, B_TILE), lambda b, c: (b, 0, c)),
        out_shape=jax.ShapeDtypeStruct((bsz, t, B_WIDTH), BF16),
        scratch_shapes=[pltpu.VMEM((t + 2 * B_PAD, B_TILE), F32)] + [pltpu.VMEM((t, B_TILE), F32)] * 4,
        compiler_params=_params(("parallel", "parallel")),
        name="rglru",
    )(proj3, proj3, wg, pk)


NA_KEYS = NA_KH * GRID_W
NA_UNROLL = 16


NA_PAIR = LANES // GRID_W


def _na_bias_tables(rpb):
    c = np.arange(GRID_W)
    cs = np.clip(c - NA_KW // 2, 0, GRID_W - NA_KW)
    kc = np.arange(GRID_W)
    inwin = (kc[None, :] >= cs[:, None]) & (kc[None, :] < cs[:, None] + NA_KW)
    off = np.clip(kc[None, :] - c[:, None] + NA_KW - 1, 0, 2 * NA_KW - 2)
    onehot = (off[None] == np.arange(2 * NA_KW - 1)[:, None, None]).astype(np.float32)
    tb = jnp.einsum("hdo,ock->hdck", rpb.astype(F32), jnp.asarray(onehot), precision=lax.Precision.HIGHEST)
    tb = jnp.where(inwin[None, None], tb, NEG)
    n_d = tb.shape[1] - (NA_PAIR - 1)
    pair = jnp.concatenate([tb[:, e:e + n_d] for e in range(NA_PAIR)], axis=-1)
    pair = pair.reshape(C_HEADS // 4, 4, n_d, GRID_W, LANES).transpose(0, 2, 1, 3, 4)
    return pair.reshape(C_HEADS // 4, n_d, 4 * GRID_W, LANES)


NA_SLOTS = 2


def _na_kernel(q_ref, k_ref, v_ref, tbl_ref, o_ref, s_ref, *, rows):
    lane_head = _lane_head(GRID_W)

    def window(r):
        rs = jnp.clip(r - NA_KH // 2, 0, rows - NA_KH)
        return rs - r + NA_KH - 1, pl.multiple_of(r * GRID_W, GRID_W), pl.multiple_of(rs * GRID_W, GRID_W)

    def scores(r):
        dr0, q0, k0 = window(r)
        qs = _stack_heads(q_ref[pl.ds(q0, GRID_W), :], lane_head)
        kw = k_ref[pl.ds(k0, NA_KEYS), :]
        s = lax.dot_general(qs, kw, (((1,), (1,)), ((), ())), preferred_element_type=F32)
        return s + jnp.concatenate([tbl_ref[dr0 + NA_PAIR * k] for k in range(NA_KH // NA_PAIR)], axis=1)

    def finish(r, s):
        _, q0, k0 = window(r)
        vw = v_ref[pl.ds(k0, NA_KEYS), :]
        m = jnp.max(s, axis=-1, keepdims=True)
        p = jnp.exp(s - m)
        l = jnp.sum(p, axis=-1, keepdims=True)
        pv = jnp.dot(p.astype(BF16), vw, preferred_element_type=F32) / l
        o = pv[0:GRID_W]
        for h in range(1, 4):
            o = jnp.where(lane_head == h, pv[h * GRID_W:(h + 1) * GRID_W], o)
        o_ref[pl.ds(q0, GRID_W), :] = o.astype(o_ref.dtype)

    s_ref[0] = scores(0)

    def body(pair, carry):
        for slot in range(NA_SLOTS):
            r = pair * NA_SLOTS + slot
            s_next = scores(jnp.minimum(r + 1, rows - 1))
            finish(r, s_ref[slot])
            s_ref[(slot + 1) % NA_SLOTS] = s_next
        return carry

    lax.fori_loop(0, rows // NA_SLOTS, body, 0, unroll=NA_UNROLL // NA_SLOTS)


def _mixer_na(proj3, tbl):
    bsz, t, _ = proj3.shape
    rows = t // GRID_W
    nhg = C_HEADS // 4

    def col_spec(off):
        return pl.BlockSpec((None, t, HEAD_GROUP_LANES), lambda hg, b: (b, 0, off + hg))

    return pl.pallas_call(
        functools.partial(_na_kernel, rows=rows),
        grid=(nhg, bsz),
        in_specs=[col_spec(COL_QC), col_spec(COL_KC), col_spec(COL_VC),
                  pl.BlockSpec((None,) + tbl.shape[1:], lambda hg, b: (hg, 0, 0, 0))],
        out_specs=pl.BlockSpec((None, t, HEAD_GROUP_LANES), lambda hg, b: (b, 0, hg)),
        out_shape=jax.ShapeDtypeStruct((bsz, t, C_WIDTH), BF16),
        scratch_shapes=[pltpu.VMEM((NA_SLOTS, 4 * GRID_W, NA_KEYS), F32)],
        compiler_params=_params(("parallel", "parallel")),
        name="na_attn",
    )(proj3, proj3, proj3, tbl)


def _tile(m, want):
    return want if m % want == 0 else m


TILE_PROJ = (1024, D_IN // 2)
TILE_MIX_OUT_ROWS = 512
TILE_SWIGLU = (4096, 512)
TILE_FFN_OUT = (1024, 512)
TILE_FFN_OUT_NORM = (512, 1024)


def _encode(x, w):
    bsz, t, d = x.shape
    m = bsz * t
    xf = x.reshape(m, d)
    for l in range(DEPTH):
        proj = _norm_proj(xf, w["norm1_g"][l], w["w_in"], l, _tile(m, TILE_PROJ[0]), TILE_PROJ[1])
        proj3 = proj.reshape(bsz, t, D_IN)
        o_a = _mixer_dilated(proj3)
        o_b = _mixer_rglru(proj3, w["rg_wg"][l], w["rg_pk"][l])
        o_c = _mixer_na(proj3, w["na_tbl"][l])
        xf, xn = _mix_out(xf, o_a.reshape(m, A_OUT), o_b.reshape(m, B_WIDTH), o_c.reshape(m, C_WIDTH),
                          w["w_out"], l, w["norm2_g"][l], _tile(m, TILE_MIX_OUT_ROWS))
        act = _swiglu(xn, w["w_ffn_in"], l, _tile(m, TILE_SWIGLU[0]), TILE_SWIGLU[1])
        if l < DEPTH - 1:
            xf = _ffn_out(xf, act, w["w_ffn_out"], l, _tile(m, TILE_FFN_OUT[0]), TILE_FFN_OUT[1])
        else:
            xf = _ffn_out_norm(xf, act, w["w_ffn_out"], l, w["final_g"], _tile(m, TILE_FFN_OUT_NORM[0]),
                               TILE_FFN_OUT_NORM[1])
    return xf.reshape(bsz, t, d)


def kernel(x_prompt, x_sample, norm1_g, w_in, conv_w, conv_b, rg_wa, rg_ba, rg_wx, rg_bx, rg_lam, na_rpb,
           w_out, norm2_g, w_ffn_in, w_ffn_out, final_g):
    rg = [_rglru_weights(conv_w[l], conv_b[l], rg_wa[l], rg_ba[l], rg_wx[l], rg_bx[l], rg_lam[l])
          for l in range(DEPTH)]
    w = {
        "norm1_g": norm1_g, "norm2_g": norm2_g, "final_g": final_g,
        "w_in": w_in.astype(BF16), "w_out": w_out.astype(BF16),
        "w_ffn_in": w_ffn_in.astype(BF16), "w_ffn_out": w_ffn_out.astype(BF16),
        "rg_wg": [r[0] for r in rg], "rg_pk": [r[1] for r in rg],
        "na_tbl": [_na_bias_tables(na_rpb[l]) for l in range(DEPTH)],
    }
    return _encode(x_prompt, w), _encode(x_sample, w)
```
